```python
import math
import jax
import jax.numpy as jnp
from jax import lax
import numpy as np

D_MODEL = 1024
BATCH = 32
SEQ = 2048
DEPTH = 1

D_FF = 2816
DN_HEADS = 8
DN_HEAD_DIM = 64
DN_WIDTH = DN_HEADS * DN_HEAD_DIM
CONV_WIDTH = 4
CHUNK = 64
S5_GROUP_CH = 16
S5_GROUPS = 32
S5_WIDTH = S5_GROUPS * S5_GROUP_CH
S5_STATE = 64
N_MOD = 9
EPS = 1e-6
IN_WIDTH = 4 * DN_WIDTH + 2 * DN_HEADS + S5_WIDTH + 2 * D_MODEL

kernel_name = 'hybrid_deltanet_s5_macaron'


def rmsnorm(x, gain):
    x32 = x.astype(jnp.float32)
    y = x32 * lax.rsqrt(jnp.mean(x32 * x32, axis=-1, keepdims=True) + EPS)
    return (y * gain.astype(jnp.float32)).astype(x.dtype)


def modulate(x, shift, scale):
    return x * (1 + scale) + shift


def swiglu(x, w1, w3, w2):
    return (jax.nn.silu(x @ w1) * (x @ w3)) @ w2


def l2norm(t):
    return t * lax.rsqrt(jnp.sum(t * t, axis=-1, keepdims=True) + EPS)


def causal_depthwise_conv(x, w):
    return lax.conv_general_dilated(
        x, w[:, None, :].astype(x.dtype), window_strides=(1,),
        padding=[(CONV_WIDTH - 1, 0)], dimension_numbers=('NWC', 'WIO', 'NWC'),
        feature_group_count=x.shape[-1])


def split_combined(p):
    sizes = (DN_WIDTH, DN_WIDTH, DN_WIDTH, DN_WIDTH, DN_HEADS, DN_HEADS, S5_WIDTH, D_MODEL, D_MODEL)
    parts = []
    start = 0
    for size in sizes:
        parts.append(p[..., start:start + size])
        start += size
    return parts


def gated_deltanet(q, k, v, z, beta_logit, decay_logit, conv_w, a_log, dt_bias, g_onorm):
    f32 = jnp.float32
    dtype = q.dtype
    bsz, seq, _ = q.shape
    n_chunks = seq // CHUNK
    qkv = jax.nn.silu(causal_depthwise_conv(jnp.concatenate([q, k, v], axis=-1), conv_w)).astype(f32)
    q, k, v = jnp.split(qkv, 3, axis=-1)

    def to_chunks(t):
        return t.reshape(bsz, n_chunks, CHUNK, DN_HEADS, DN_HEAD_DIM).transpose(0, 3, 1, 2, 4)

    def to_chunks_h(t):
        return t.reshape(bsz, n_chunks, CHUNK, DN_HEADS).transpose(0, 3, 1, 2)

    q = l2norm(to_chunks(q)) * (DN_HEAD_DIM ** -0.5)
    k = l2norm(to_chunks(k))
    v = to_chunks(v)
    beta = to_chunks_h(jax.nn.sigmoid(beta_logit.astype(f32)))
    log_alpha = -jnp.exp(a_log.astype(f32)) * jax.nn.softplus(decay_logit.astype(f32) + dt_bias.astype(f32))
    g_cum = jnp.cumsum(to_chunks_h(log_alpha), axis=-1)
    causal = jnp.tril(jnp.ones((CHUNK, CHUNK), dtype=bool))
    strict = jnp.tril(jnp.ones((CHUNK, CHUNK), dtype=bool), k=-1)
    decay = jnp.exp(jnp.where(causal, g_cum[..., :, None] - g_cum[..., None, :], -jnp.inf))
    k_beta = k * beta[..., None]
    kk = jnp.where(strict, jnp.einsum('bhnik,bhnjk->bhnij', k_beta, k) * decay, 0.0)
    lhs = kk + jnp.eye(CHUNK, dtype=f32)
    rhs = jnp.concatenate([v * beta[..., None], k_beta * jnp.exp(g_cum)[..., None]], axis=-1)
    sol = lax.linalg.triangular_solve(lhs, rhs, left_side=True, lower=True, unit_diagonal=True)
    u_c, w_c = sol[..., :DN_HEAD_DIM], sol[..., DN_HEAD_DIM:]
    attn = jnp.einsum('bhnik,bhnjk->bhnij', q, k) * decay
    q_dec = q * jnp.exp(g_cum)[..., None]
    g_last = g_cum[..., -1]
    k_dec = k * jnp.exp(g_last[..., None] - g_cum)[..., None]

    def step(state, xs):
        q_i, k_i, u_i, w_i, a_i, gl_i = xs
        v_new = u_i - jnp.einsum('bhck,bhkv->bhcv', w_i, state)
        o_i = jnp.einsum('bhck,bhkv->bhcv', q_i, state) + jnp.einsum('bhij,bhjv->bhiv', a_i, v_new)
        state = state * jnp.exp(gl_i)[..., None, None] + jnp.einsum('bhck,bhcv->bhkv', k_i, v_new)
        return state, o_i

    xs = (q_dec, k_dec, u_c, w_c, attn, g_last)
    xs = tuple(jnp.moveaxis(t, 2, 0) for t in xs)
    state0 = jnp.zeros((bsz, DN_HEADS, DN_HEAD_DIM, DN_HEAD_DIM), f32)
    _, o = lax.scan(step, state0, xs)
    o = o.transpose(1, 0, 3, 2, 4).reshape(bsz, seq, DN_HEADS, DN_HEAD_DIM)
    gate = jax.nn.silu(z.astype(f32)).reshape(bsz, seq, DN_HEADS, DN_HEAD_DIM)
    o = o * lax.rsqrt(jnp.mean(o * o, axis=-1, keepdims=True) + EPS) * g_onorm.astype(f32) * gate
    return o.reshape(bsz, seq, DN_WIDTH).astype(dtype)


def s5_ssm(u_in, lam_re, lam_im, log_step, b_re, b_im, c_re, c_im, d_skip, w_glu, b_glu):
    f32 = jnp.float32
    dtype = u_in.dtype
    bsz, seq, _ = u_in.shape
    u = u_in.astype(f32).reshape(bsz, seq, S5_GROUPS, S5_GROUP_CH)
    lam_re = jnp.minimum(lam_re.astype(f32), -1e-4)
    lam_im = lam_im.astype(f32)
    step = jnp.exp(log_step.astype(f32))[:, None]
    mag = jnp.exp(lam_re * step)
    ang = lam_im * step
    lb_re = mag * jnp.cos(ang)
    lb_im = mag * jnp.sin(ang)
    den = lam_re * lam_re + lam_im * lam_im
    coef_re = ((lb_re - 1.0) * lam_re + lb_im * lam_im) / den
    coef_im = (lb_im * lam_re - (lb_re - 1.0) * lam_im) / den
    b_re = b_re.astype(f32)
    b_im = b_im.astype(f32)
    bb_re = coef_re[..., None] * b_re - coef_im[..., None] * b_im
    bb_im = coef_re[..., None] * b_im + coef_im[..., None] * b_re
    bu_re = jnp.einsum('bsgc,gpc->bsgp', u, bb_re)
    bu_im = jnp.einsum('bsgc,gpc->bsgp', u, bb_im)
    a_re = jnp.broadcast_to(lb_re, (1, seq, S5_GROUPS, S5_STATE))
    a_im = jnp.broadcast_to(lb_im, (1, seq, S5_GROUPS, S5_STATE))

    def combine(e1, e2):
        a1r, a1i, b1r, b1i = e1
        a2r, a2i, b2r, b2i = e2
        return (a2r * a1r - a2i * a1i,
                a2r * a1i + a2i * a1r,
                a2r * b1r - a2i * b1i + b2r,
                a2r * b1i + a2i * b1r + b2i)

    _, _, x_re, x_im = lax.associative_scan(combine, (a_re, a_im, bu_re, bu_im), axis=1)
    y = (jnp.einsum('bsgp,gcp->bsgc', x_re, c_re.astype(f32))
         - jnp.einsum('bsgp,gcp->bsgc', x_im, c_im.astype(f32))
         + d_skip.astype(f32).reshape(S5_GROUPS, S5_GROUP_CH) * u)
    y = jax.nn.gelu(y.reshape(bsz, seq, S5_WIDTH))
    y = y * jax.nn.sigmoid(y @ w_glu.astype(f32) + b_glu.astype(f32))
    return y.astype(dtype)


def hybrid_layer(h, c, w_ada, b_ada, g_ffn1, w1_ffn1, w3_ffn1, w2_ffn1, g_mix, w_in, conv_qkv,
                 a_log, dt_bias, g_onorm, lam_re, lam_im, log_step, b_re, b_im, c_re, c_im,
                 d_skip, w_glu, b_glu, w_proj_a, w_proj_b, w_out, g_ffn2, w1_ffn2, w3_ffn2, w2_ffn2):
    mod = jax.nn.silu(c) @ w_ada + b_ada
    sh1, sc1, gt1, sh2, sc2, gt2, sh3, sc3, gt3 = [m[:, None, :] for m in jnp.split(mod, N_MOD, axis=-1)]
    h = h + 0.5 * gt1 * swiglu(modulate(rmsnorm(h, g_ffn1), sh1, sc1), w1_ffn1, w3_ffn1, w2_ffn1)
    u = modulate(rmsnorm(h, g_mix), sh2, sc2)
    q, k, v, z, beta_logit, decay_logit, s5_in, gate_a, gate_b = split_combined(u @ w_in)
    y_a = gated_deltanet(q, k, v, z, beta_logit, decay_logit, conv_qkv, a_log, dt_bias, g_onorm) @ w_proj_a
    y_b = s5_ssm(s5_in, lam_re, lam_im, log_step, b_re, b_im, c_re, c_im, d_skip, w_glu, b_glu) @ w_proj_b
    merged = jax.nn.sigmoid(gate_a) * y_a + jax.nn.sigmoid(gate_b) * y_b
    h = h + gt2 * (merged @ w_out)
    h = h + 0.5 * gt3 * swiglu(modulate(rmsnorm(h, g_ffn2), sh3, sc3), w1_ffn2, w3_ffn2, w2_ffn2)
    return h


def setup_inputs(seed: int = 0) -> dict:
    key = jax.random.key(seed)
    ks = jax.random.split(key, 40)
    f32 = jnp.float32
    L = DEPTH

    def nrm(k, shape, scale):
        return jax.random.normal(k, shape, f32) * scale

    def log_uniform(k, shape, lo, hi):
        return jax.random.uniform(k, shape, f32, math.log(lo), math.log(hi))

    dt = jnp.exp(log_uniform(ks[12], (L, DN_HEADS), 1e-3, 1e-1))
    n_idx = jnp.arange(S5_STATE, dtype=f32)
    return {
        'x': nrm(ks[0], (BATCH, SEQ, D_MODEL), 1.0),
        'c': nrm(ks[1], (BATCH, D_MODEL), 1.0),
        'w_ada': nrm(ks[2], (L, D_MODEL, N_MOD * D_MODEL), 0.5 * D_MODEL ** -0.5),
        'b_ada': nrm(ks[3], (L, N_MOD * D_MODEL), 0.02),
        'g_ffn1': 1.0 + nrm(ks[4], (L, D_MODEL), 0.02),
        'w1_ffn1': nrm(ks[5], (L, D_MODEL, D_FF), D_MODEL ** -0.5),
        'w3_ffn1': nrm(ks[6], (L, D_MODEL, D_FF), D_MODEL ** -0.5),
        'w2_ffn1': nrm(ks[7], (L, D_FF, D_MODEL), D_FF ** -0.5),
        'g_mix': 1.0 + nrm(ks[8], (L, D_MODEL), 0.02),
        'w_in': nrm(ks[9], (L, D_MODEL, IN_WIDTH), D_MODEL ** -0.5),
        'conv_qkv': nrm(ks[10], (L, CONV_WIDTH, 3 * DN_WIDTH), CONV_WIDTH ** -0.5),
        'a_log': jnp.log(jax.random.uniform(ks[11], (L, DN_HEADS), f32, 1.0, 16.0)),
        'dt_bias': dt + jnp.log(-jnp.expm1(-dt)),
        'g_onorm': 1.0 + nrm(ks[13], (L, DN_HEAD_DIM), 0.02),
        'lam_re': -0.5 + nrm(ks[14], (L, S5_GROUPS, S5_STATE), 0.01),
        'lam_im': math.pi * n_idx + nrm(ks[15], (L, S5_GROUPS, S5_STATE), 0.01),
        'log_step': log_uniform(ks[16], (L, S5_GROUPS), 1e-3, 1e-1),
        'b_re': nrm(ks[17], (L, S5_GROUPS, S5_STATE, S5_GROUP_CH), (2 * S5_GROUP_CH) ** -0.5),
        'b_im': nrm(ks[18], (L, S5_GROUPS, S5_STATE, S5_GROUP_CH), (2 * S5_GROUP_CH) ** -0.5),
        'c_re': nrm(ks[19], (L, S5_GROUPS, S5_GROUP_CH, S5_STATE), S5_STATE ** -0.5),
        'c_im': nrm(ks[20], (L, S5_GROUPS, S5_GROUP_CH, S5_STATE), S5_STATE ** -0.5),
        'd_skip': nrm(ks[21], (L, S5_WIDTH), 1.0),
        'w_glu': nrm(ks[22], (L, S5_WIDTH, S5_WIDTH), S5_WIDTH ** -0.5),
        'b_glu': nrm(ks[23], (L, S5_WIDTH), 0.02),
        'w_proj_a': nrm(ks[24], (L, DN_WIDTH, D_MODEL), DN_WIDTH ** -0.5),
        'w_proj_b': nrm(ks[25], (L, S5_WIDTH, D_MODEL), S5_WIDTH ** -0.5),
        'w_out': nrm(ks[26], (L, D_MODEL, D_MODEL), D_MODEL ** -0.5),
        'g_ffn2': 1.0 + nrm(ks[27], (L, D_MODEL), 0.02),
        'w1_ffn2': nrm(ks[28], (L, D_MODEL, D_FF), D_MODEL ** -0.5),
        'w3_ffn2': nrm(ks[29], (L, D_MODEL, D_FF), D_MODEL ** -0.5),
        'w2_ffn2': nrm(ks[30], (L, D_FF, D_MODEL), D_FF ** -0.5),
        'g_final': 1.0 + nrm(ks[31], (D_MODEL,), 0.02),
    }


def reference(x, c, w_ada, b_ada, g_ffn1, w1_ffn1, w3_ffn1, w2_ffn1, g_mix, w_in, conv_qkv,
              a_log, dt_bias, g_onorm, lam_re, lam_im, log_step, b_re, b_im, c_re, c_im,
              d_skip, w_glu, b_glu, w_proj_a, w_proj_b, w_out, g_ffn2, w1_ffn2, w3_ffn2, w2_ffn2,
              g_final):
    h = x
    for layer in range(DEPTH):
        h = hybrid_layer(
            h, c, w_ada[layer], b_ada[layer], g_ffn1[layer], w1_ffn1[layer], w3_ffn1[layer],
            w2_ffn1[layer], g_mix[layer], w_in[layer], conv_qkv[layer], a_log[layer],
            dt_bias[layer], g_onorm[layer], lam_re[layer], lam_im[layer], log_step[layer],
            b_re[layer], b_im[layer], c_re[layer], c_im[layer], d_skip[layer], w_glu[layer],
            b_glu[layer], w_proj_a[layer], w_proj_b[layer], w_out[layer], g_ffn2[layer],
            w1_ffn2[layer], w3_ffn2[layer], w2_ffn2[layer])
    return rmsnorm(h, g_final)
```

```python
import functools
import math

import jax
import jax.numpy as jnp
from jax import lax
from jax.experimental import pallas as pl
from jax.experimental.pallas import tpu as pltpu

F32 = jnp.float32
BF16 = jnp.bfloat16
EPS = 1e-6

LANES = 128
SUBLANES = 8
VMEM_LIMIT_BYTES = 56 * 1024 * 1024

N_MOD = 9
DN_HEADS = 8
DN_HEAD_DIM = 64
DN_WIDTH = DN_HEADS * DN_HEAD_DIM
CONV_WIDTH = 4
CHUNK = 64
N_PAIRS = DN_WIDTH // LANES
S5_GROUP_CH = 16
S5_STATE = 64
S5_WIDTH = 512
S5_GB = S5_WIDTH // LANES
S5_GPB = LANES // S5_GROUP_CH
S5_L = 8
S5_SW = S5_GPB * S5_STATE

FFN_TM = 512
INPROJ_TM = 512
DN_TS = 256
MERGE_TM = 512


def _dot(a, b):
    return jnp.dot(a, b, preferred_element_type=F32)


def _dot_nt(a, b):
    return lax.dot_general(a, b, (((1,), (1,)), ((), ())), preferred_element_type=F32)


def _dot_tn(a, b):
    return lax.dot_general(a, b, (((0,), (0,)), ((), ())), preferred_element_type=F32)


def _split3(x):
    hi = x.astype(BF16)
    r = x - hi.astype(F32)
    mid = r.astype(BF16)
    lo = (r - mid.astype(F32)).astype(BF16)
    return hi, mid, lo


def _dot_sel_r(x, sel):
    hi, mid, lo = _split3(x)
    return _dot(hi, sel) + _dot(mid, sel) + _dot(lo, sel)


def _dot_sel_l(sel, x):
    hi, mid, lo = _split3(x)
    return _dot(sel, hi) + _dot(sel, mid) + _dot(sel, lo)


def _silu(x):
    return x * jax.nn.sigmoid(x)


def _norm_mod(x, gain, shift, scale):
    ms = jnp.mean(x * x, axis=-1, keepdims=True)
    y = x * lax.rsqrt(ms + EPS) * gain
    return y * (1.0 + scale) + shift


def _cparams(sem):
    return pltpu.CompilerParams(dimension_semantics=sem, vmem_limit_bytes=VMEM_LIMIT_BYTES)


def _resident(shape):
    nd = len(shape)
    return pl.BlockSpec(shape, lambda *_: (0,) * nd, pipeline_mode=pl.Buffered(1))


def _mod_kernel(c_ref, w_ref, b_ref, o_ref):
    a = _silu(c_ref[...])
    o_ref[...] = jnp.dot(a, w_ref[...], preferred_element_type=F32,
                         precision=lax.Precision.HIGHEST) + b_ref[...]


def _mod_call(c, w_ada, b_ada):
    bsz, d = c.shape
    n = w_ada.shape[1]
    return pl.pallas_call(
        _mod_kernel,
        grid=(n // d,),
        in_specs=[pl.BlockSpec((bsz, d), lambda j: (0, 0)),
                  pl.BlockSpec((d, d), lambda j: (0, j)),
                  pl.BlockSpec((1, d), lambda j: (0, j))],
        out_specs=pl.BlockSpec((bsz, d), lambda j: (0, j)),
        out_shape=jax.ShapeDtypeStruct((bsz, n), F32),
        compiler_params=_cparams(("arbitrary",)),
        name="mod",
    )(c, w_ada, b_ada.reshape(1, n))


def _ffn_kernel(x_ref, mod_ref, g_ref, w1_ref, w3_ref, w2_ref, gf_ref, o_ref, *, row, final):
    x = x_ref[0]
    shift = mod_ref[0, row:row + 1, :]
    scale = mod_ref[0, row + 1:row + 2, :]
    gate = mod_ref[0, row + 2:row + 3, :]
    u = _norm_mod(x, g_ref[...], shift, scale).astype(BF16)
    a = (_silu(_dot(u, w1_ref[...])) * _dot(u, w3_ref[...])).astype(BF16)
    h = x + 0.5 * gate * _dot(a, w2_ref[...])
    if final:
        ms = jnp.mean(h * h, axis=-1, keepdims=True)
        h = h * lax.rsqrt(ms + EPS) * gf_ref[...]
    o_ref[0] = h


def _ffn_call(x, mod, gain, w1, w3, w2, g_final, *, row, final):
    bsz, seq, d = x.shape
    f = w1.shape[1]
    tm = min(FFN_TM, seq)
    return pl.pallas_call(
        functools.partial(_ffn_kernel, row=row, final=final),
        grid=(bsz, seq // tm),
        in_specs=[pl.BlockSpec((1, tm, d), lambda b, i: (b, i, 0)),
                  pl.BlockSpec((1, N_MOD, d), lambda b, i: (b, 0, 0)),
                  _resident((1, d)), _resident((d, f)), _resident((d, f)), _resident((f, d)),
                  _resident((1, d))],
        out_specs=pl.BlockSpec((1, tm, d), lambda b, i: (b, i, 0)),
        out_shape=jax.ShapeDtypeStruct((bsz, seq, d), F32),
        compiler_params=_cparams(("parallel", "parallel")),
        name="ffn_final" if final else "ffn",
    )(x, mod, gain.reshape(1, d), w1, w3, w2, g_final.reshape(1, d))


_W_QKV = 3 * DN_WIDTH
_OFF_Z = _W_QKV
_OFF_BA = _OFF_Z + DN_WIDTH
_OFF_S5 = _OFF_BA + LANES
_OFF_G = _OFF_S5 + S5_WIDTH


def _inproj_kernel(h_ref, mod_ref, g_ref, w_ref, qkv_ref, z_ref, ba_ref, s5_ref, gab_ref):
    u = _norm_mod(h_ref[0], g_ref[...], mod_ref[0, 3:4, :], mod_ref[0, 4:5, :]).astype(BF16)
    qkv_ref[0] = _dot(u, w_ref[:, 0:_OFF_Z])
    z_ref[0] = _dot(u, w_ref[:, _OFF_Z:_OFF_BA])
    ba_ref[0] = _dot(u, w_ref[:, _OFF_BA:_OFF_S5])
    s5_ref[0] = _dot(u, w_ref[:, _OFF_S5:_OFF_G])
    gab_ref[0] = _dot(u, w_ref[:, _OFF_G:])


def _inproj_call(h, mod, gain, w_cat):
    bsz, seq, d = h.shape
    tm = min(INPROJ_TM, seq)
    wg = w_cat.shape[1] - _OFF_G
    widths = (_W_QKV, DN_WIDTH, LANES, S5_WIDTH, wg)
    return pl.pallas_call(
        _inproj_kernel,
        grid=(bsz, seq // tm),
        in_specs=[pl.BlockSpec((1, tm, d), lambda b, i: (b, i, 0)),
                  pl.BlockSpec((1, N_MOD, d), lambda b, i: (b, 0, 0)),
                  _resident((1, d)), _resident(w_cat.shape)],
        out_specs=[pl.BlockSpec((1, tm, w), lambda b, i: (b, i, 0)) for w in widths],
        out_shape=[jax.ShapeDtypeStruct((bsz, seq, w), F32) for w in widths],
        compiler_params=_cparams(("parallel", "parallel")),
        name="inproj",
    )(h, mod, gain.reshape(1, d), w_cat)


def _seg_sum(x, seg):
    outs = []
    for p in range(N_PAIRS):
        xp = x[:, p * LANES:(p + 1) * LANES]
        hi = xp.astype(BF16)
        lo = (xp - hi.astype(F32)).astype(BF16)
        outs.append(_dot(hi, seg) + _dot(lo, seg))
    return jnp.concatenate(outs, axis=1)


def _dn_kernel(qkv_ref, ba_ref, z_ref, cw_ref, hp_ref, gon_ref, o_ref,
               xbuf, st_ref, qn_s, kn_s, kb_s, kbg_s, vb_s, qd_s, kd_s, gc_s, egl_s, o_s, *, ts):
    w = DN_WIDTH
    hd = DN_HEAD_DIM
    n_chunks = ts // CHUNK

    @pl.when(pl.program_id(1) == 0)
    def _():
        xbuf[0:SUBLANES, :] = jnp.zeros((SUBLANES, 3 * w), F32)
        st_ref[...] = jnp.zeros(st_ref.shape, F32)

    xbuf[SUBLANES:SUBLANES + ts, :] = qkv_ref[0]
    acc = cw_ref[CONV_WIDTH - 1:CONV_WIDTH, :] * xbuf[SUBLANES:SUBLANES + ts, :]
    for j in range(CONV_WIDTH - 1):
        off = SUBLANES - (CONV_WIDTH - 1) + j
        acc = acc + cw_ref[j:j + 1, :] * xbuf[off:off + ts, :]
    xbuf[0:SUBLANES, :] = xbuf[ts:ts + SUBLANES, :]
    act = _silu(acc)
    q = act[:, 0:w]
    k = act[:, w:2 * w]
    v = act[:, 2 * w:3 * w]

    r128 = lax.broadcasted_iota(jnp.int32, (LANES, LANES), 0)
    c128 = lax.broadcasted_iota(jnp.int32, (LANES, LANES), 1)
    bd_mask = (r128 // hd) == (c128 // hd)
    seg = jnp.where(bd_mask, 1.0, 0.0).astype(BF16)
    er = lax.broadcasted_iota(jnp.int32, (LANES, w), 0)
    ec = lax.broadcasted_iota(jnp.int32, (LANES, w), 1) // hd
    exp_beta = jnp.where(er == ec, 1.0, 0.0).astype(BF16)
    exp_alpha = jnp.where(er - DN_HEADS == ec, 1.0, 0.0).astype(BF16)
    tr = lax.broadcasted_iota(jnp.int32, (ts, ts), 0)
    tc = lax.broadcasted_iota(jnp.int32, (ts, ts), 1)
    same_chunk = (tr // CHUNK) == (tc // CHUNK)
    tri_cum = jnp.where(same_chunk & (tr >= tc), 1.0, 0.0).astype(BF16)
    blk_ones = jnp.where(same_chunk, 1.0, 0.0).astype(BF16)

    ba = ba_ref[0]
    beta = jax.nn.sigmoid(ba)
    xa = ba + hp_ref[1:2, :]
    softplus = jnp.maximum(xa, 0.0) + jnp.log1p(jnp.exp(-jnp.abs(xa)))
    log_alpha = -jnp.exp(hp_ref[0:1, :]) * softplus
    la_hi, la_mid, la_lo = _split3(log_alpha)
    g_cum = _dot(tri_cum, la_hi) + _dot(tri_cum, la_mid) + _dot(tri_cum, la_lo)
    g_last = _dot(blk_ones, la_hi) + _dot(blk_ones, la_mid) + _dot(blk_ones, la_lo)
    beta_f = _dot_sel_r(beta, exp_beta)
    gc_f = _dot_sel_r(g_cum, exp_alpha)
    gl_f = _dot_sel_r(g_last, exp_alpha)

    qn = q * (lax.rsqrt(_seg_sum(q * q, seg) + EPS) * (hd ** -0.5))
    kn = k * lax.rsqrt(_seg_sum(k * k, seg) + EPS)
    kb = kn * beta_f
    eg = jnp.exp(gc_f)
    qn_s[...] = qn.astype(BF16)
    kn_s[...] = kn.astype(BF16)
    kb_s[...] = kb.astype(BF16)
    kbg_s[...] = (kb * eg).astype(BF16)
    vb_s[...] = (v * beta_f).astype(BF16)
    qd_s[...] = (qn * eg).astype(BF16)
    kd_s[...] = (kn * jnp.exp(gl_f - gc_f)).astype(BF16)
    gc_s[...] = gc_f
    egl_s[...] = jnp.exp(gl_f)

    ri = lax.broadcasted_iota(jnp.int32, (CHUNK, LANES), 0)
    li = lax.broadcasted_iota(jnp.int32, (CHUNK, LANES), 1)
    jj = li % hd
    causal = ri >= jj
    strict = ri > jj
    eye = ri == jj
    eye_f = jnp.where(eye, 1.0, 0.0)
    in16 = (ri // 16) == (jj // 16)
    in32 = (ri // 32) == (jj // 32)
    head0 = li < hd

    def bd(y):
        yb = y.astype(BF16)
        return jnp.where(bd_mask, jnp.concatenate([yb, yb], axis=0), jnp.zeros((), BF16))

    def mm(x, y):
        return _dot(x.astype(BF16), bd(y))

    def chunk_body(c, carry):
        r0 = pl.multiple_of(c * CHUNK, CHUNK)
        rows = pl.ds(r0, CHUNK)
        for p in range(N_PAIRS):
            lanes = slice(p * LANES, (p + 1) * LANES)
            kn_c = kn_s[rows, lanes]
            zero_b = jnp.zeros((), BF16)
            k2 = jnp.concatenate([jnp.where(head0, kn_c, zero_b), jnp.where(head0, zero_b, kn_c)], axis=0)
            lhs2 = jnp.concatenate([kb_s[rows, lanes], qn_s[rows, lanes]], axis=0)
            p2 = _dot_nt(lhs2, k2)
            gc_c = gc_s[rows, lanes]
            gc_row = jnp.sum(jnp.where(eye, gc_c, 0.0), axis=0, keepdims=True)
            decay = jnp.exp(jnp.where(causal, gc_c - gc_row, -1e30))
            a_mat = jnp.where(strict, p2[0:CHUNK] * decay, 0.0)
            attn = p2[CHUNK:2 * CHUNK] * decay

            d = jnp.where(in16, a_mat, 0.0)
            d2 = mm(d, d)
            d4 = mm(d2, d2)
            d8 = mm(d4, d4)
            imd = eye_f - d
            p1 = imd + mm(imd, d2)
            ipd4 = eye_f + d4
            pb = ipd4 + mm(ipd4, d8)
            t16 = mm(p1, pb)
            l32 = jnp.where(in32 & jnp.logical_not(in16), a_mat, 0.0)
            t32 = t16 - mm(mm(t16, l32), t16)
            l64 = jnp.where(in32, 0.0, a_mat)
            t64 = t32 - mm(mm(t32, l64), t32)

            rhs = jnp.concatenate([bd(vb_s[rows, lanes]), bd(kbg_s[rows, lanes])], axis=1)
            uw = _dot(t64.astype(BF16), rhs)
            u_c = uw[:, 0:LANES]
            w_c = uw[:, LANES:2 * LANES]

            s_old = st_ref[p]
            lhs_s = jnp.concatenate([w_c.astype(BF16), qd_s[rows, lanes]], axis=0)
            rs = _dot(lhs_s, s_old.astype(BF16))
            v_new = u_c - rs[0:CHUNK]
            o_c = rs[CHUNK:2 * CHUNK] + _dot(attn.astype(BF16), bd(v_new))
            upd = _dot_tn(kd_s[rows, lanes], v_new.astype(BF16))
            st_ref[p] = s_old * egl_s[pl.ds(r0, 1), lanes] + jnp.where(bd_mask, upd, 0.0)
            o_s[rows, lanes] = o_c
        return carry

    lax.fori_loop(0, n_chunks, chunk_body, 0)

    o = o_s[...]
    ms = _seg_sum(o * o, seg) * (1.0 / hd)
    out = o * lax.rsqrt(ms + EPS) * gon_ref[...] * _silu(z_ref[0])
    o_ref[0] = out.astype(o_ref.dtype)


def _dn_call(qkv, ba, z, conv_w, hp, gon):
    bsz, seq, _ = qkv.shape
    ts = min(DN_TS, seq)
    w = DN_WIDTH
    bf = lambda: pltpu.VMEM((ts, w), BF16)
    ff = lambda: pltpu.VMEM((ts, w), F32)
    return pl.pallas_call(
        functools.partial(_dn_kernel, ts=ts),
        grid=(bsz, seq // ts),
        in_specs=[pl.BlockSpec((1, ts, 3 * w), lambda b, i: (b, i, 0)),
                  pl.BlockSpec((1, ts, LANES), lambda b, i: (b, i, 0)),
                  pl.BlockSpec((1, ts, w), lambda b, i: (b, i, 0)),
                  _resident((CONV_WIDTH, 3 * w)), _resident((SUBLANES, LANES)), _resident((1, w))],
        out_specs=pl.BlockSpec((1, ts, w), lambda b, i: (b, i, 0)),
        out_shape=jax.ShapeDtypeStruct((bsz, seq, w), BF16),
        scratch_shapes=[pltpu.VMEM((ts + SUBLANES, 3 * w), F32),
                        pltpu.VMEM((N_PAIRS, LANES, LANES), F32),
                        bf(), bf(), bf(), bf(), bf(), bf(), bf(), ff(), ff(), ff()],
        compiler_params=_cparams(("parallel", "arbitrary")),
        name="deltanet",
    )(qkv, ba, z, conv_w, hp, gon)


def _s5_weights(lam_re, lam_im, log_step, b_re, b_im, c_re, c_im):
    hp = lax.Precision.HIGHEST
    g, p = lam_re.shape
    gc = b_re.shape[-1]
    el = S5_L
    lr = jnp.minimum(lam_re.astype(F32), -1e-4)
    li = lam_im.astype(F32)
    step = jnp.exp(log_step.astype(F32))[:, None]
    mag = jnp.exp(lr * step)
    ang = li * step
    lb_re = mag * jnp.cos(ang)
    lb_im = mag * jnp.sin(ang)
    den = lr * lr + li * li
    coef_re = ((lb_re - 1.0) * lr + lb_im * li) / den
    coef_im = (lb_im * lr - (lb_re - 1.0) * li) / den
    b_re = b_re.astype(F32)
    b_im = b_im.astype(F32)
    bb_re = coef_re[..., None] * b_re - coef_im[..., None] * b_im
    bb_im = coef_re[..., None] * b_im + coef_im[..., None] * b_re
    tau = jnp.arange(el + 1, dtype=F32)[:, None, None]
    pw_mag = jnp.exp(tau * (lr * step)[None])
    pw_re = pw_mag * jnp.cos(tau * ang[None])
    pw_im = pw_mag * jnp.sin(tau * ang[None])
    c_re = c_re.astype(F32)
    c_im = c_im.astype(F32)
    cp_re = c_re[None] * pw_re[:, :, None, :] - c_im[None] * pw_im[:, :, None, :]
    cp_im = c_re[None] * pw_im[:, :, None, :] + c_im[None] * pw_re[:, :, None, :]
    kern = (jnp.einsum('tgcp,gpd->tgcd', cp_re[:el], bb_re, precision=hp)
            - jnp.einsum('tgcp,gpd->tgcd', cp_im[:el], bb_im, precision=hp))
    eye_g = jnp.eye(S5_GPB, dtype=F32)
    t_in = jnp.arange(el)[:, None]
    t_out = jnp.arange(el)[None, :]
    lag = jnp.clip(t_out - t_in, 0, el - 1)
    toe = jnp.where((t_out >= t_in)[:, :, None, None, None], kern[lag], 0.0)
    toe = toe.reshape(el, el, S5_GB, S5_GPB, gc, gc)
    w_t = jnp.einsum('tubgoi,gh->btgiuho', toe, eye_g).reshape(S5_GB, el * LANES, el * LANES)
    tau_rev = (el - 1.0) - tau[:el]
    rev_mag = jnp.exp(tau_rev * (lr * step)[None])
    rev_re = rev_mag * jnp.cos(tau_rev * ang[None])
    rev_im = rev_mag * jnp.sin(tau_rev * ang[None])
    in_re = rev_re[..., None] * bb_re[None] - rev_im[..., None] * bb_im[None]
    in_im = rev_re[..., None] * bb_im[None] + rev_im[..., None] * bb_re[None]
    w_in = jnp.stack([in_re, in_im], axis=0).reshape(2, el, S5_GB, S5_GPB, p, gc)
    w_in = jnp.einsum('rtbgpi,gh->btgirhp', w_in, eye_g).reshape(S5_GB, el * LANES, 2 * S5_SW)
    w_o = jnp.stack([cp_re[1:], -cp_im[1:]], axis=0).reshape(2, el, S5_GB, S5_GPB, gc, p)
    w_o = jnp.einsum('rubgop,gh->brgpuho', w_o, eye_g).reshape(S5_GB, 2 * S5_SW, el * LANES)
    lam = jnp.stack([pw_re[el], pw_im[el]], axis=0).reshape(2, S5_GB, S5_SW).transpose(1, 0, 2)
    w_a = jnp.concatenate([w_t, w_in], axis=2)
    return w_a.astype(BF16), w_o.astype(BF16), lam


def _s5_kernel(u_ref, wa_ref, wo_ref, lam_ref, d_ref, y_ref, s_s, xp_s, *, nrow):
    el = S5_L
    sw = S5_SW
    xs = [u_ref[0, pl.ds(t, nrow, stride=el), :] for t in range(el)]
    xcat = jnp.concatenate([x.astype(BF16) for x in xs], axis=1)
    r = _dot(xcat, wa_ref[0])
    s_s[...] = r[:, el * LANES:]
    lam_r = lam_ref[0, 0:1, :]
    lam_i = lam_ref[0, 1:2, :]

    def step(n, carry):
        x_r, x_i = carry
        xp_s[pl.ds(n, 1), 0:sw] = x_r
        xp_s[pl.ds(n, 1), sw:2 * sw] = x_i
        s_r = s_s[pl.ds(n, 1), 0:sw]
        s_i = s_s[pl.ds(n, 1), sw:2 * sw]
        return (lam_r * x_r - lam_i * x_i + s_r, lam_r * x_i + lam_i * x_r + s_i)

    zero = jnp.zeros((1, sw), F32)
    lax.fori_loop(0, nrow, step, (zero, zero))
    y = r[:, 0:el * LANES] + _dot(xp_s[...].astype(BF16), wo_ref[0])
    dsk = d_ref[0]
    for t in range(el):
        y_ref[0, pl.ds(t, nrow, stride=el), :] = y[:, t * LANES:(t + 1) * LANES] + dsk * xs[t]


def _s5_call(u, w_a, w_o, lam, d_skip):
    bsz, seq, width = u.shape
    nrow = seq // S5_L
    return pl.pallas_call(
        functools.partial(_s5_kernel, nrow=nrow),
        grid=(S5_GB, bsz),
        in_specs=[pl.BlockSpec((1, seq, LANES), lambda g, b: (b, 0, g)),
                  pl.BlockSpec((1,) + w_a.shape[1:], lambda g, b: (g, 0, 0)),
                  pl.BlockSpec((1,) + w_o.shape[1:], lambda g, b: (g, 0, 0)),
                  pl.BlockSpec((1, 2, S5_SW), lambda g, b: (g, 0, 0)),
                  pl.BlockSpec((1, 1, LANES), lambda g, b: (g, 0, 0))],
        out_specs=pl.BlockSpec((1, seq, LANES), lambda g, b: (b, 0, g)),
        out_shape=jax.ShapeDtypeStruct((bsz, seq, width), F32),
        scratch_shapes=[pltpu.VMEM((nrow, 2 * S5_SW), F32), pltpu.VMEM((nrow, 2 * S5_SW), F32)],
        compiler_params=_cparams(("arbitrary", "arbitrary")),
        name="s5",
    )(u, w_a, w_o, lam, d_skip.reshape(S5_GB, 1, LANES))


def _gelu_tanh(x):
    return 0.5 * x * (1.0 + jnp.tanh(math.sqrt(2.0 / math.pi) * (x + 0.044715 * (x * x * x))))


def _merge_kernel(h_ref, oa_ref, ys_ref, gab_ref, mod_ref, wglu_ref, bglu_ref, wpa_ref, wpb_ref, wout_ref, o_ref):
    d = h_ref.shape[-1]
    yb = _gelu_tanh(ys_ref[0])
    yb = yb * jax.nn.sigmoid(_dot(yb.astype(BF16), wglu_ref[...]) + bglu_ref[...])
    y_b = _dot(yb.astype(BF16), wpb_ref[...])
    y_a = _dot(oa_ref[0], wpa_ref[...])
    gab = gab_ref[0]
    merged = jax.nn.sigmoid(gab[:, 0:d]) * y_a + jax.nn.sigmoid(gab[:, d:2 * d]) * y_b
    o_ref[0] = h_ref[0] + mod_ref[0, 5:6, :] * _dot(merged.astype(BF16), wout_ref[...])


def _merge_call(h, o_a, y_s, gab, mod, w_glu, b_glu, w_pa, w_pb, w_out):
    bsz, seq, d = h.shape
    tm = min(MERGE_TM, seq)
    tok = lambda w: pl.BlockSpec((1, tm, w), lambda b, i: (b, i, 0))
    return pl.pallas_call(
        _merge_kernel,
        grid=(bsz, seq // tm),
        in_specs=[tok(d), tok(o_a.shape[-1]), tok(y_s.shape[-1]), tok(gab.shape[-1]),
                  pl.BlockSpec((1, N_MOD, d), lambda b, i: (b, 0, 0)),
                  _resident(w_glu.shape), _resident((1, b_glu.shape[-1])), _resident(w_pa.shape),
                  _resident(w_pb.shape), _resident(w_out.shape)],
        out_specs=tok(d),
        out_shape=jax.ShapeDtypeStruct((bsz, seq, d), F32),
        compiler_params=_cparams(("parallel", "parallel")),
        name="merge",
    )(h, o_a, y_s, gab, mod, w_glu, b_glu.reshape(1, -1), w_pa, w_pb, w_out)


def _regroup_w_in(w_in):
    w = DN_WIDTH
    ba = w_in[:, 4 * w:4 * w + 2 * DN_HEADS]
    ba = jnp.pad(ba, ((0, 0), (0, LANES - 2 * DN_HEADS)))
    s5_off = 4 * w + 2 * DN_HEADS
    return jnp.concatenate([w_in[:, 0:3 * w], w_in[:, 3 * w:4 * w], ba,
                            w_in[:, s5_off:s5_off + S5_WIDTH], w_in[:, s5_off + S5_WIDTH:]], axis=1).astype(BF16)


def kernel(x, c, w_ada, b_ada, g_ffn1, w1_ffn1, w3_ffn1, w2_ffn1, g_mix, w_in, conv_qkv, a_log, dt_bias, g_onorm, lam_re, lam_im, log_step, b_re, b_im, c_re, c_im, d_skip, w_glu, b_glu, w_proj_a, w_proj_b, w_out, g_ffn2, w1_ffn2, w3_ffn2, w2_ffn2, g_final):
    bsz, seq, d = x.shape
    depth = w_ada.shape[0]
    h = x
    for l in range(depth):
        last = l == depth - 1
        mod = _mod_call(c, w_ada[l], b_ada[l]).reshape(bsz, N_MOD, d)
        h = _ffn_call(h, mod, g_ffn1[l], w1_ffn1[l].astype(BF16), w3_ffn1[l].astype(BF16),
                      w2_ffn1[l].astype(BF16), g_final, row=0, final=False)
        qkv, z, ba, s5_in, gab = _inproj_call(h, mod, g_mix[l], _regroup_w_in(w_in[l]))
        hp = jnp.zeros((SUBLANES, LANES), F32)
        hp = hp.at[0, DN_HEADS:2 * DN_HEADS].set(a_log[l]).at[1, DN_HEADS:2 * DN_HEADS].set(dt_bias[l])
        gon = jnp.tile(g_onorm[l], DN_HEADS).reshape(1, DN_WIDTH)
        o_a = _dn_call(qkv, ba, z, conv_qkv[l], hp, gon)
        w_a, w_o, lam = _s5_weights(lam_re[l], lam_im[l], log_step[l], b_re[l], b_im[l], c_re[l], c_im[l])
        y_s = _s5_call(s5_in, w_a, w_o, lam, d_skip[l])
        h = _merge_call(h, o_a, y_s, gab, mod, w_glu[l].astype(BF16), b_glu[l], w_proj_a[l].astype(BF16),
                        w_proj_b[l].astype(BF16), w_out[l].astype(BF16))
        h = _ffn_call(h, mod, g_ffn2[l], w1_ffn2[l].astype(BF16), w3_ffn2[l].astype(BF16),
                      w2_ffn2[l].astype(BF16), g_final, row=6, final=last)
    if depth == 0:
        raise ValueError("depth must be positive")
    return h
```

```python
import functools
import math

import jax
import jax.numpy as jnp
from jax import lax
from jax.experimental import pallas as pl
from jax.experimental.pallas import tpu as pltpu

F32 = jnp.float32
BF16 = jnp.bfloat16
EPS = 1e-6

LANES = 128
SUBLANES = 8
VMEM_LIMIT_BYTES = 56 * 1024 * 1024

N_MOD = 9
DN_HEADS = 8
DN_HEAD_DIM = 64
DN_WIDTH = DN_HEADS * DN_HEAD_DIM
CONV_WIDTH = 4
CHUNK = 64
N_PAIRS = DN_WIDTH // LANES
S5_GROUP_CH = 16
S5_STATE = 64
S5_WIDTH = 512
S5_GB = S5_WIDTH // LANES
S5_GPB = LANES // S5_GROUP_CH
S5_L = 8
S5_SW = S5_GPB * S5_STATE

FFN_TM = 512
INPROJ_TM = 512
DN_TS = 256
MERGE_TM = 512


def _dot(a, b):
    return jnp.dot(a, b, preferred_element_type=F32)


def _dot_nt(a, b):
    return lax.dot_general(a, b, (((1,), (1,)), ((), ())), preferred_element_type=F32)


def _dot_tn(a, b):
    return lax.dot_general(a, b, (((0,), (0,)), ((), ())), preferred_element_type=F32)


def _split3(x):
    hi = x.astype(BF16)
    r = x - hi.astype(F32)
    mid = r.astype(BF16)
    lo = (r - mid.astype(F32)).astype(BF16)
    return hi, mid, lo


def _dot_sel_r(x, sel):
    hi, mid, lo = _split3(x)
    return _dot(hi, sel) + _dot(mid, sel) + _dot(lo, sel)


def _dot_sel_l(sel, x):
    hi, mid, lo = _split3(x)
    return _dot(sel, hi) + _dot(sel, mid) + _dot(sel, lo)


def _silu(x):
    return x * jax.nn.sigmoid(x)


def _norm_mod(x, gain, shift, scale):
    ms = jnp.mean(x * x, axis=-1, keepdims=True)
    y = x * lax.rsqrt(ms + EPS) * gain
    return y * (1.0 + scale) + shift


def _cparams(sem):
    return pltpu.CompilerParams(dimension_semantics=sem, vmem_limit_bytes=VMEM_LIMIT_BYTES)


def _resident(shape):
    nd = len(shape)
    return pl.BlockSpec(shape, lambda *_: (0,) * nd, pipeline_mode=pl.Buffered(1))


def _mod_kernel(c_ref, w_ref, b_ref, o_ref):
    a = _silu(c_ref[...])
    o_ref[...] = jnp.dot(a, w_ref[...], preferred_element_type=F32,
                         precision=lax.Precision.HIGHEST) + b_ref[...]


def _mod_call(c, w_ada, b_ada):
    bsz, d = c.shape
    n = w_ada.shape[1]
    return pl.pallas_call(
        _mod_kernel,
        grid=(n // d,),
        in_specs=[pl.BlockSpec((bsz, d), lambda j: (0, 0)),
                  pl.BlockSpec((d, d), lambda j: (0, j)),
                  pl.BlockSpec((1, d), lambda j: (0, j))],
        out_specs=pl.BlockSpec((bsz, d), lambda j: (0, j)),
        out_shape=jax.ShapeDtypeStruct((bsz, n), F32),
        compiler_params=_cparams(("arbitrary",)),
        name="mod",
    )(c, w_ada, b_ada.reshape(1, n))


def _ffn_kernel(x_ref, mod_ref, g_ref, w1_ref, w3_ref, w2_ref, gf_ref, o_ref, *, row, final):
    x = x_ref[0]
    shift = mod_ref[0, row:row + 1, :]
    scale = mod_ref[0, row + 1:row + 2, :]
    gate = mod_ref[0, row + 2:row + 3, :]
    u = _norm_mod(x, g_ref[...], shift, scale).astype(BF16)
    a = (_silu(_dot(u, w1_ref[...])) * _dot(u, w3_ref[...])).astype(BF16)
    h = x + 0.5 * gate * _dot(a, w2_ref[...])
    if final:
        ms = jnp.mean(h * h, axis=-1, keepdims=True)
        h = h * lax.rsqrt(ms + EPS) * gf_ref[...]
    o_ref[0] = h


def _ffn_call(x, mod, gain, w1, w3, w2, g_final, *, row, final):
    bsz, seq, d = x.shape
    f = w1.shape[1]
    tm = min(FFN_TM, seq)
    return pl.pallas_call(
        functools.partial(_ffn_kernel, row=row, final=final),
        grid=(bsz, seq // tm),
        in_specs=[pl.BlockSpec((1, tm, d), lambda b, i: (b, i, 0)),
                  pl.BlockSpec((1, N_MOD, d), lambda b, i: (b, 0, 0)),
                  _resident((1, d)), _resident((d, f)), _resident((d, f)), _resident((f, d)),
                  _resident((1, d))],
        out_specs=pl.BlockSpec((1, tm, d), lambda b, i: (b, i, 0)),
        out_shape=jax.ShapeDtypeStruct((bsz, seq, d), F32),
        compiler_params=_cparams(("parallel", "parallel")),
        name="ffn_final" if final else "ffn",
    )(x, mod, gain.reshape(1, d), w1, w3, w2, g_final.reshape(1, d))


_W_QKV = 3 * DN_WIDTH
_OFF_Z = _W_QKV
_OFF_BA = _OFF_Z + DN_WIDTH
_OFF_S5 = _OFF_BA + LANES
_OFF_G = _OFF_S5 + S5_WIDTH


def _inproj_kernel(h_ref, mod_ref, g_ref, w_ref, qkv_ref, z_ref, ba_ref, s5_ref, gab_ref):
    u = _norm_mod(h_ref[0], g_ref[...], mod_ref[0, 3:4, :], mod_ref[0, 4:5, :]).astype(BF16)
    qkv_ref[0] = _dot(u, w_ref[:, 0:_OFF_Z])
    z_ref[0] = _dot(u, w_ref[:, _OFF_Z:_OFF_BA]).astype(z_ref.dtype)
    ba_ref[0] = _dot(u, w_ref[:, _OFF_BA:_OFF_S5])
    s5_ref[0] = _dot(u, w_ref[:, _OFF_S5:_OFF_G])
    gab_ref[0] = _dot(u, w_ref[:, _OFF_G:]).astype(gab_ref.dtype)


def _inproj_call(h, mod, gain, w_cat):
    bsz, seq, d = h.shape
    tm = min(INPROJ_TM, seq)
    wg = w_cat.shape[1] - _OFF_G
    widths = (_W_QKV, DN_WIDTH, LANES, S5_WIDTH, wg)
    dtypes = (F32, BF16, F32, F32, BF16)
    return pl.pallas_call(
        _inproj_kernel,
        grid=(bsz, seq // tm),
        in_specs=[pl.BlockSpec((1, tm, d), lambda b, i: (b, i, 0)),
                  pl.BlockSpec((1, N_MOD, d), lambda b, i: (b, 0, 0)),
                  _resident((1, d)), _resident(w_cat.shape)],
        out_specs=[pl.BlockSpec((1, tm, w), lambda b, i: (b, i, 0)) for w in widths],
        out_shape=[jax.ShapeDtypeStruct((bsz, seq, w), t) for w, t in zip(widths, dtypes)],
        compiler_params=_cparams(("parallel", "parallel")),
        name="inproj",
    )(h, mod, gain.reshape(1, d), w_cat)


def _seg_sum(x, seg):
    outs = []
    for p in range(N_PAIRS):
        xp = x[:, p * LANES:(p + 1) * LANES]
        hi = xp.astype(BF16)
        lo = (xp - hi.astype(F32)).astype(BF16)
        outs.append(_dot(hi, seg) + _dot(lo, seg))
    return jnp.concatenate(outs, axis=1)


def _dn_kernel(qkv_ref, ba_ref, z_ref, cw_ref, hp_ref, gon_ref, o_ref,
               xbuf, st_ref, qn_s, kn_s, kb_s, kbg_s, vb_s, qd_s, kd_s, w_s, attn_s, gc_s, egl_s, o_s, u_s, *, ts):
    w = DN_WIDTH
    hd = DN_HEAD_DIM
    n_chunks = ts // CHUNK

    @pl.when(pl.program_id(1) == 0)
    def _():
        xbuf[0:SUBLANES, :] = jnp.zeros((SUBLANES, 3 * w), F32)
        st_ref[...] = jnp.zeros(st_ref.shape, F32)

    xbuf[SUBLANES:SUBLANES + ts, :] = qkv_ref[0]
    acc = cw_ref[CONV_WIDTH - 1:CONV_WIDTH, :] * xbuf[SUBLANES:SUBLANES + ts, :]
    for j in range(CONV_WIDTH - 1):
        off = SUBLANES - (CONV_WIDTH - 1) + j
        acc = acc + cw_ref[j:j + 1, :] * xbuf[off:off + ts, :]
    xbuf[0:SUBLANES, :] = xbuf[ts:ts + SUBLANES, :]
    act = _silu(acc)
    q = act[:, 0:w]
    k = act[:, w:2 * w]
    v = act[:, 2 * w:3 * w]

    r128 = lax.broadcasted_iota(jnp.int32, (LANES, LANES), 0)
    c128 = lax.broadcasted_iota(jnp.int32, (LANES, LANES), 1)
    bd_mask = (r128 // hd) == (c128 // hd)
    seg = jnp.where(bd_mask, 1.0, 0.0).astype(BF16)
    er = lax.broadcasted_iota(jnp.int32, (LANES, w), 0)
    ec = lax.broadcasted_iota(jnp.int32, (LANES, w), 1) // hd
    exp_beta = jnp.where(er == ec, 1.0, 0.0).astype(BF16)
    exp_alpha = jnp.where(er - DN_HEADS == ec, 1.0, 0.0).astype(BF16)
    tr = lax.broadcasted_iota(jnp.int32, (ts, ts), 0)
    tc = lax.broadcasted_iota(jnp.int32, (ts, ts), 1)
    same_chunk = (tr // CHUNK) == (tc // CHUNK)
    tri_cum = jnp.where(same_chunk & (tr >= tc), 1.0, 0.0).astype(BF16)
    blk_ones = jnp.where(same_chunk, 1.0, 0.0).astype(BF16)

    ba = ba_ref[0]
    beta = jax.nn.sigmoid(ba)
    xa = ba + hp_ref[1:2, :]
    softplus = jnp.maximum(xa, 0.0) + jnp.log1p(jnp.exp(-jnp.abs(xa)))
    log_alpha = -jnp.exp(hp_ref[0:1, :]) * softplus
    la_hi, la_mid, la_lo = _split3(log_alpha)
    g_cum = _dot(tri_cum, la_hi) + _dot(tri_cum, la_mid) + _dot(tri_cum, la_lo)
    g_last = _dot(blk_ones, la_hi) + _dot(blk_ones, la_mid) + _dot(blk_ones, la_lo)
    beta_f = _dot_sel_r(beta, exp_beta)
    gc_f = _dot_sel_r(g_cum, exp_alpha)
    gl_f = _dot_sel_r(g_last, exp_alpha)

    qn = q * (lax.rsqrt(_seg_sum(q * q, seg) + EPS) * (hd ** -0.5))
    kn = k * lax.rsqrt(_seg_sum(k * k, seg) + EPS)
    kb = kn * beta_f
    eg = jnp.exp(gc_f)
    qn_s[...] = qn.astype(BF16)
    kn_s[...] = kn.astype(BF16)
    kb_s[...] = kb.astype(BF16)
    kbg_s[...] = (kb * eg).astype(BF16)
    vb_s[...] = (v * beta_f).astype(BF16)
    qd_s[...] = (qn * eg).astype(BF16)
    kd_s[...] = (kn * jnp.exp(gl_f - gc_f)).astype(BF16)
    gc_s[...] = gc_f
    egl_s[...] = jnp.exp(gl_f)

    ri = lax.broadcasted_iota(jnp.int32, (CHUNK, LANES), 0)
    li = lax.broadcasted_iota(jnp.int32, (CHUNK, LANES), 1)
    jj = li % hd
    causal = ri >= jj
    strict = ri > jj
    eye = ri == jj
    eye_f = jnp.where(eye, 1.0, 0.0)
    in16 = (ri // 16) == (jj // 16)
    in32 = (ri // 32) == (jj // 32)
    head0 = li < hd

    def bd(y):
        yb = y.astype(BF16)
        return jnp.where(bd_mask, jnp.concatenate([yb, yb], axis=0), jnp.zeros((), BF16))

    def mm(x, y):
        return _dot(x.astype(BF16), bd(y))

    def tile(c, p):
        return (slice(c * CHUNK, (c + 1) * CHUNK), slice(p * LANES, (p + 1) * LANES))

    items = [tile(c, p) for c in range(n_chunks) for p in range(N_PAIRS)]
    zero_b = jnp.zeros((), BF16)
    kn_c = [kn_s[it] for it in items]
    k2 = [jnp.concatenate([jnp.where(head0, x, zero_b), jnp.where(head0, zero_b, x)], axis=0) for x in kn_c]
    p2 = [_dot_nt(jnp.concatenate([kb_s[it], qn_s[it]], axis=0), y) for it, y in zip(items, k2)]
    a_mat = []
    for it, pp in zip(items, p2):
        gc_c = gc_s[it]
        gc_row = jnp.sum(jnp.where(eye, gc_c, 0.0), axis=0, keepdims=True)
        decay = jnp.exp(jnp.where(causal, gc_c - gc_row, -1e30))
        a_mat.append(jnp.where(strict, pp[0:CHUNK] * decay, 0.0))
        attn_s[it] = (pp[CHUNK:2 * CHUNK] * decay).astype(BF16)

    d = [jnp.where(in16, a, 0.0) for a in a_mat]
    d2 = [mm(x, x) for x in d]
    d4 = [mm(x, x) for x in d2]
    d8 = [mm(x, x) for x in d4]
    imd = [eye_f - x for x in d]
    p1 = [x + mm(x, y) for x, y in zip(imd, d2)]
    ipd4 = [eye_f + x for x in d4]
    pb = [x + mm(x, y) for x, y in zip(ipd4, d8)]
    t16 = [mm(x, y) for x, y in zip(p1, pb)]
    in32_only = in32 & jnp.logical_not(in16)
    f32_ = [mm(t, jnp.where(in32_only, a, 0.0)) for t, a in zip(t16, a_mat)]
    t32 = [t - mm(f, t) for t, f in zip(t16, f32_)]
    f64_ = [mm(t, jnp.where(in32, 0.0, a)) for t, a in zip(t32, a_mat)]
    t64 = [t - mm(f, t) for t, f in zip(t32, f64_)]
    for it, t in zip(items, t64):
        rhs = jnp.concatenate([bd(vb_s[it]), bd(kbg_s[it])], axis=1)
        uw = _dot(t.astype(BF16), rhs)
        u_s[it] = uw[:, 0:LANES]
        w_s[it] = uw[:, LANES:2 * LANES].astype(BF16)

    state = [st_ref[p] for p in range(N_PAIRS)]
    for c in range(n_chunks):
        its = [tile(c, p) for p in range(N_PAIRS)]
        rs = [_dot(jnp.concatenate([w_s[it], qd_s[it]], axis=0), s.astype(BF16))
              for it, s in zip(its, state)]
        v_new = [u_s[it] - r[0:CHUNK] for it, r in zip(its, rs)]
        upd = [_dot_tn(kd_s[it], v.astype(BF16)) for it, v in zip(its, v_new)]
        o_c = [r[CHUNK:2 * CHUNK] + _dot(attn_s[it], bd(v)) for it, r, v in zip(its, rs, v_new)]
        state = [s * egl_s[c * CHUNK:c * CHUNK + 1, it[1]] + jnp.where(bd_mask, x, 0.0)
                 for it, s, x in zip(its, state, upd)]
        for it, o in zip(its, o_c):
            o_s[it] = o
    for p in range(N_PAIRS):
        st_ref[p] = state[p]

    o = o_s[...]
    ms = _seg_sum(o * o, seg) * (1.0 / hd)
    out = o * lax.rsqrt(ms + EPS) * gon_ref[...] * _silu(z_ref[0].astype(F32))
    o_ref[0] = out.astype(o_ref.dtype)


def _dn_call(qkv, ba, z, conv_w, hp, gon):
    bsz, seq, _ = qkv.shape
    ts = min(DN_TS, seq)
    w = DN_WIDTH
    bf = lambda: pltpu.VMEM((ts, w), BF16)
    ff = lambda: pltpu.VMEM((ts, w), F32)
    return pl.pallas_call(
        functools.partial(_dn_kernel, ts=ts),
        grid=(bsz, seq // ts),
        in_specs=[pl.BlockSpec((1, ts, 3 * w), lambda b, i: (b, i, 0)),
                  pl.BlockSpec((1, ts, LANES), lambda b, i: (b, i, 0)),
                  pl.BlockSpec((1, ts, w), lambda b, i: (b, i, 0)),
                  _resident((CONV_WIDTH, 3 * w)), _resident((SUBLANES, LANES)), _resident((1, w))],
        out_specs=pl.BlockSpec((1, ts, w), lambda b, i: (b, i, 0)),
        out_shape=jax.ShapeDtypeStruct((bsz, seq, w), BF16),
        scratch_shapes=[pltpu.VMEM((ts + SUBLANES, 3 * w), F32),
                        pltpu.VMEM((N_PAIRS, LANES, LANES), F32),
                        bf(), bf(), bf(), bf(), bf(), bf(), bf(), bf(), bf(), ff(), ff(), ff(), ff()],
        compiler_params=_cparams(("parallel", "arbitrary")),
        name="deltanet",
    )(qkv, ba, z, conv_w, hp, gon)


def _s5_weights(lam_re, lam_im, log_step, b_re, b_im, c_re, c_im):
    hp = lax.Precision.HIGHEST
    g, p = lam_re.shape
    gc = b_re.shape[-1]
    el = S5_L
    lr = jnp.minimum(lam_re.astype(F32), -1e-4)
    li = lam_im.astype(F32)
    step = jnp.exp(log_step.astype(F32))[:, None]
    mag = jnp.exp(lr * step)
    ang = li * step
    lb_re = mag * jnp.cos(ang)
    lb_im = mag * jnp.sin(ang)
    den = lr * lr + li * li
    coef_re = ((lb_re - 1.0) * lr + lb_im * li) / den
    coef_im = (lb_im * lr - (lb_re - 1.0) * li) / den
    b_re = b_re.astype(F32)
    b_im = b_im.astype(F32)
    bb_re = coef_re[..., None] * b_re - coef_im[..., None] * b_im
    bb_im = coef_re[..., None] * b_im + coef_im[..., None] * b_re
    tau = jnp.arange(el + 1, dtype=F32)[:, None, None]
    pw_mag = jnp.exp(tau * (lr * step)[None])
    pw_re = pw_mag * jnp.cos(tau * ang[None])
    pw_im = pw_mag * jnp.sin(tau * ang[None])
    c_re = c_re.astype(F32)
    c_im = c_im.astype(F32)
    cp_re = c_re[None] * pw_re[:, :, None, :] - c_im[None] * pw_im[:, :, None, :]
    cp_im = c_re[None] * pw_im[:, :, None, :] + c_im[None] * pw_re[:, :, None, :]
    kern = (jnp.einsum('tgcp,gpd->tgcd', cp_re[:el], bb_re, precision=hp)
            - jnp.einsum('tgcp,gpd->tgcd', cp_im[:el], bb_im, precision=hp))
    eye_g = jnp.eye(S5_GPB, dtype=F32)
    t_in = jnp.arange(el)[:, None]
    t_out = jnp.arange(el)[None, :]
    lag = jnp.clip(t_out - t_in, 0, el - 1)
    toe = jnp.where((t_out >= t_in)[:, :, None, None, None], kern[lag], 0.0)
    toe = toe.reshape(el, el, S5_GB, S5_GPB, gc, gc)
    w_t = jnp.einsum('tubgoi,gh->btgiuho', toe, eye_g).reshape(S5_GB, el * LANES, el * LANES)
    tau_rev = (el - 1.0) - tau[:el]
    rev_mag = jnp.exp(tau_rev * (lr * step)[None])
    rev_re = rev_mag * jnp.cos(tau_rev * ang[None])
    rev_im = rev_mag * jnp.sin(tau_rev * ang[None])
    in_re = rev_re[..., None] * bb_re[None] - rev_im[..., None] * bb_im[None]
    in_im = rev_re[..., None] * bb_im[None] + rev_im[..., None] * bb_re[None]
    w_in = jnp.stack([in_re, in_im], axis=0).reshape(2, el, S5_GB, S5_GPB, p, gc)
    w_in = jnp.einsum('rtbgpi,gh->btgirhp', w_in, eye_g).reshape(S5_GB, el * LANES, 2 * S5_SW)
    w_o = jnp.stack([cp_re[1:], -cp_im[1:]], axis=0).reshape(2, el, S5_GB, S5_GPB, gc, p)
    w_o = jnp.einsum('rubgop,gh->brgpuho', w_o, eye_g).reshape(S5_GB, 2 * S5_SW, el * LANES)
    lam = jnp.stack([pw_re[el], pw_im[el]], axis=0).reshape(2, S5_GB, S5_SW).transpose(1, 0, 2)
    w_a = jnp.concatenate([w_t, w_in], axis=2)
    return w_a.astype(BF16), w_o.astype(BF16), lam


def _s5_kernel(u_ref, wa_ref, wo_ref, lam_ref, d_ref, y_ref, s_s, xp_s, *, nrow):
    el = S5_L
    sw = S5_SW
    xs = [u_ref[0, pl.ds(t, nrow, stride=el), :] for t in range(el)]
    xcat = jnp.concatenate([x.astype(BF16) for x in xs], axis=1)
    r = _dot(xcat, wa_ref[0])
    s_s[...] = r[:, el * LANES:]
    lam_r = lam_ref[0, 0:1, :]
    lam_i = lam_ref[0, 1:2, :]

    def step(n, carry):
        x_r, x_i = carry
        xp_s[pl.ds(n, 1), 0:sw] = x_r
        xp_s[pl.ds(n, 1), sw:2 * sw] = x_i
        s_r = s_s[pl.ds(n, 1), 0:sw]
        s_i = s_s[pl.ds(n, 1), sw:2 * sw]
        return (lam_r * x_r - lam_i * x_i + s_r, lam_r * x_i + lam_i * x_r + s_i)

    zero = jnp.zeros((1, sw), F32)
    lax.fori_loop(0, nrow, step, (zero, zero))
    y = r[:, 0:el * LANES] + _dot(xp_s[...].astype(BF16), wo_ref[0])
    dsk = d_ref[0]
    for t in range(el):
        y_ref[0, pl.ds(t, nrow, stride=el), :] = y[:, t * LANES:(t + 1) * LANES] + dsk * xs[t]


def _s5_call(u, w_a, w_o, lam, d_skip):
    bsz, seq, width = u.shape
    nrow = seq // S5_L
    return pl.pallas_call(
        functools.partial(_s5_kernel, nrow=nrow),
        grid=(S5_GB, bsz),
        in_specs=[pl.BlockSpec((1, seq, LANES), lambda g, b: (b, 0, g)),
                  pl.BlockSpec((1,) + w_a.shape[1:], lambda g, b: (g, 0, 0)),
                  pl.BlockSpec((1,) + w_o.shape[1:], lambda g, b: (g, 0, 0)),
                  pl.BlockSpec((1, 2, S5_SW), lambda g, b: (g, 0, 0)),
                  pl.BlockSpec((1, 1, LANES), lambda g, b: (g, 0, 0))],
        out_specs=pl.BlockSpec((1, seq, LANES), lambda g, b: (b, 0, g)),
        out_shape=jax.ShapeDtypeStruct((bsz, seq, width), F32),
        scratch_shapes=[pltpu.VMEM((nrow, 2 * S5_SW), F32), pltpu.VMEM((nrow, 2 * S5_SW), F32)],
        compiler_params=_cparams(("arbitrary", "arbitrary")),
        name="s5",
    )(u, w_a, w_o, lam, d_skip.reshape(S5_GB, 1, LANES))


def _gelu_tanh(x):
    return 0.5 * x * (1.0 + jnp.tanh(math.sqrt(2.0 / math.pi) * (x + 0.044715 * (x * x * x))))


def _merge_kernel(h_ref, oa_ref, ys_ref, gab_ref, mod_ref, wglu_ref, bglu_ref, wpa_ref, wpb_ref, wout_ref, o_ref):
    d = h_ref.shape[-1]
    yb = _gelu_tanh(ys_ref[0])
    yb = yb * jax.nn.sigmoid(_dot(yb.astype(BF16), wglu_ref[...]) + bglu_ref[...])
    y_b = _dot(yb.astype(BF16), wpb_ref[...])
    y_a = _dot(oa_ref[0], wpa_ref[...])
    gab = gab_ref[0].astype(F32)
    merged = jax.nn.sigmoid(gab[:, 0:d]) * y_a + jax.nn.sigmoid(gab[:, d:2 * d]) * y_b
    o_ref[0] = h_ref[0] + mod_ref[0, 5:6, :] * _dot(merged.astype(BF16), wout_ref[...])


def _merge_call(h, o_a, y_s, gab, mod, w_glu, b_glu, w_pa, w_pb, w_out):
    bsz, seq, d = h.shape
    tm = min(MERGE_TM, seq)
    tok = lambda w: pl.BlockSpec((1, tm, w), lambda b, i: (b, i, 0))
    return pl.pallas_call(
        _merge_kernel,
        grid=(bsz, seq // tm),
        in_specs=[tok(d), tok(o_a.shape[-1]), tok(y_s.shape[-1]), tok(gab.shape[-1]),
                  pl.BlockSpec((1, N_MOD, d), lambda b, i: (b, 0, 0)),
                  _resident(w_glu.shape), _resident((1, b_glu.shape[-1])), _resident(w_pa.shape),
                  _resident(w_pb.shape), _resident(w_out.shape)],
        out_specs=tok(d),
        out_shape=jax.ShapeDtypeStruct((bsz, seq, d), F32),
        compiler_params=_cparams(("parallel", "parallel")),
        name="merge",
    )(h, o_a, y_s, gab, mod, w_glu, b_glu.reshape(1, -1), w_pa, w_pb, w_out)


def _regroup_w_in(w_in):
    w = DN_WIDTH
    ba = w_in[:, 4 * w:4 * w + 2 * DN_HEADS]
    ba = jnp.pad(ba, ((0, 0), (0, LANES - 2 * DN_HEADS)))
    s5_off = 4 * w + 2 * DN_HEADS
    return jnp.concatenate([w_in[:, 0:3 * w], w_in[:, 3 * w:4 * w], ba,
                            w_in[:, s5_off:s5_off + S5_WIDTH], w_in[:, s5_off + S5_WIDTH:]], axis=1).astype(BF16)


def kernel(x, c, w_ada, b_ada, g_ffn1, w1_ffn1, w3_ffn1, w2_ffn1, g_mix, w_in, conv_qkv, a_log, dt_bias, g_onorm, lam_re, lam_im, log_step, b_re, b_im, c_re, c_im, d_skip, w_glu, b_glu, w_proj_a, w_proj_b, w_out, g_ffn2, w1_ffn2, w3_ffn2, w2_ffn2, g_final):
    bsz, seq, d = x.shape
    depth = w_ada.shape[0]
    h = x
    for l in range(depth):
        last = l == depth - 1
        mod = _mod_call(c, w_ada[l], b_ada[l]).reshape(bsz, N_MOD, d)
        h = _ffn_call(h, mod, g_ffn1[l], w1_ffn1[l].astype(BF16), w3_ffn1[l].astype(BF16),
                      w2_ffn1[l].astype(BF16), g_final, row=0, final=False)
        qkv, z, ba, s5_in, gab = _inproj_call(h, mod, g_mix[l], _regroup_w_in(w_in[l]))
        hp = jnp.zeros((SUBLANES, LANES), F32)
        hp = hp.at[0, DN_HEADS:2 * DN_HEADS].set(a_log[l]).at[1, DN_HEADS:2 * DN_HEADS].set(dt_bias[l])
        gon = jnp.tile(g_onorm[l], DN_HEADS).reshape(1, DN_WIDTH)
        o_a = _dn_call(qkv, ba, z, conv_qkv[l], hp, gon)
        w_a, w_o, lam = _s5_weights(lam_re[l], lam_im[l], log_step[l], b_re[l], b_im[l], c_re[l], c_im[l])
        y_s = _s5_call(s5_in, w_a, w_o, lam, d_skip[l])
        h = _merge_call(h, o_a, y_s, gab, mod, w_glu[l].astype(BF16), b_glu[l], w_proj_a[l].astype(BF16),
                        w_proj_b[l].astype(BF16), w_out[l].astype(BF16))
        h = _ffn_call(h, mod, g_ffn2[l], w1_ffn2[l].astype(BF16), w3_ffn2[l].astype(BF16),
                      w2_ffn2[l].astype(BF16), g_final, row=6, final=last)
    if depth == 0:
        raise ValueError("depth must be positive")
    return h
```

```python
import functools
import math

import jax
import jax.numpy as jnp
from jax import lax
from jax.experimental import pallas as pl
from jax.experimental.pallas import tpu as pltpu

F32 = jnp.float32
BF16 = jnp.bfloat16
EPS = 1e-6

LANES = 128
SUBLANES = 8
VMEM_LIMIT_BYTES = 56 * 1024 * 1024

N_MOD = 9
DN_HEADS = 8
DN_HEAD_DIM = 64
DN_WIDTH = DN_HEADS * DN_HEAD_DIM
CONV_WIDTH = 4
CHUNK = 64
N_PAIRS = DN_WIDTH // LANES
S5_GROUP_CH = 16
S5_STATE = 64
S5_WIDTH = 512
S5_GB = S5_WIDTH // LANES
S5_GPB = LANES // S5_GROUP_CH
S5_L = 8
S5_SW = S5_GPB * S5_STATE

FFN_TM = 512
INPROJ_TM = 512
INPROJ_CB = 256
DN_TS = 256
DN_NB = 4
MERGE_TM = 512


def _dot(a, b):
    return jnp.dot(a, b, preferred_element_type=F32)


def _dot_nt(a, b):
    return lax.dot_general(a, b, (((1,), (1,)), ((), ())), preferred_element_type=F32)


def _dot_tn(a, b):
    return lax.dot_general(a, b, (((0,), (0,)), ((), ())), preferred_element_type=F32)


def _split3(x):
    hi = x.astype(BF16)
    r = x - hi.astype(F32)
    mid = r.astype(BF16)
    lo = (r - mid.astype(F32)).astype(BF16)
    return hi, mid, lo


def _dot_sel_r(x, sel):
    hi = x.astype(BF16)
    lo = (x - hi.astype(F32)).astype(BF16)
    return _dot(hi, sel) + _dot(lo, sel)


def _silu(x):
    return x * jax.nn.sigmoid(x)


def _sigmoid(x):
    return 0.5 * jnp.tanh(0.5 * x) + 0.5


def _norm_mod(x, gain, shift, scale):
    ms = jnp.mean(x * x, axis=-1, keepdims=True)
    y = x * lax.rsqrt(ms + EPS) * gain
    return y * (1.0 + scale) + shift


def _cparams(sem):
    return pltpu.CompilerParams(dimension_semantics=sem, vmem_limit_bytes=VMEM_LIMIT_BYTES)


def _resident(shape):
    nd = len(shape)
    return pl.BlockSpec(shape, lambda *_: (0,) * nd, pipeline_mode=pl.Buffered(1))


def _mod_kernel(c_ref, w_ref, b_ref, o_ref):
    a = _silu(c_ref[...])
    o_ref[...] = jnp.dot(a, w_ref[...], preferred_element_type=F32,
                         precision=lax.Precision.HIGHEST) + b_ref[...]


def _mod_call(c, w_ada, b_ada):
    bsz, d = c.shape
    n = w_ada.shape[1]
    return pl.pallas_call(
        _mod_kernel,
        grid=(n // d,),
        in_specs=[pl.BlockSpec((bsz, d), lambda j: (0, 0)),
                  pl.BlockSpec((d, d), lambda j: (0, j)),
                  pl.BlockSpec((1, d), lambda j: (0, j))],
        out_specs=pl.BlockSpec((bsz, d), lambda j: (0, j)),
        out_shape=jax.ShapeDtypeStruct((bsz, n), F32),
        compiler_params=_cparams(("arbitrary",)),
        name="mod",
    )(c, w_ada, b_ada.reshape(1, n))


def _ffn_kernel(x_ref, mod_ref, g_ref, w1_ref, w3_ref, w2_ref, gf_ref, o_ref, *, row, final):
    x = x_ref[0]
    shift = mod_ref[0, row:row + 1, :]
    scale = mod_ref[0, row + 1:row + 2, :]
    gate = mod_ref[0, row + 2:row + 3, :]
    u = _norm_mod(x, g_ref[...], shift, scale).astype(BF16)
    a = (_silu(_dot(u, w1_ref[...])) * _dot(u, w3_ref[...])).astype(BF16)
    h = x + 0.5 * gate * _dot(a, w2_ref[...])
    if final:
        ms = jnp.mean(h * h, axis=-1, keepdims=True)
        h = h * lax.rsqrt(ms + EPS) * gf_ref[...]
    o_ref[0] = h


def _ffn_call(x, mod, gain, w1, w3, w2, g_final, *, row, final):
    bsz, seq, d = x.shape
    f = w1.shape[1]
    tm = min(FFN_TM, seq)
    return pl.pallas_call(
        functools.partial(_ffn_kernel, row=row, final=final),
        grid=(bsz, seq // tm),
        in_specs=[pl.BlockSpec((1, tm, d), lambda b, i: (b, i, 0)),
                  pl.BlockSpec((1, N_MOD, d), lambda b, i: (b, 0, 0)),
                  _resident((1, d)), _resident((d, f)), _resident((d, f)), _resident((f, d)),
                  _resident((1, d))],
        out_specs=pl.BlockSpec((1, tm, d), lambda b, i: (b, i, 0)),
        out_shape=jax.ShapeDtypeStruct((bsz, seq, d), F32),
        compiler_params=_cparams(("parallel", "parallel")),
        name="ffn_final" if final else "ffn",
    )(x, mod, gain.reshape(1, d), w1, w3, w2, g_final.reshape(1, d))


_W_QKV = 3 * DN_WIDTH
_OFF_Z = _W_QKV
_OFF_BA = _OFF_Z + DN_WIDTH
_OFF_S5 = _OFF_BA + LANES
_OFF_G = _OFF_S5 + S5_WIDTH


def _inproj_kernel(h_ref, mod_ref, g_ref, w_ref, cw_ref, qkv_ref, z_ref, ba_ref, s5_ref, gab_ref, *xbufs, tm):
    @pl.when(pl.program_id(1) == 0)
    def _():
        for xbuf in xbufs:
            xbuf[0:SUBLANES, :] = jnp.zeros((SUBLANES, INPROJ_CB), F32)

    u = _norm_mod(h_ref[0], g_ref[...], mod_ref[0, 3:4, :], mod_ref[0, 4:5, :]).astype(BF16)

    def project(c0):
        xbuf = xbufs[c0 // INPROJ_CB]
        xbuf[SUBLANES:SUBLANES + tm, :] = _dot(u, w_ref[:, c0:c0 + INPROJ_CB])

    def conv_silu(c0):
        xbuf = xbufs[c0 // INPROJ_CB]
        cols = slice(c0, c0 + INPROJ_CB)
        acc = cw_ref[CONV_WIDTH - 1:CONV_WIDTH, cols] * xbuf[SUBLANES:SUBLANES + tm, :]
        for t in range(CONV_WIDTH - 1):
            off = SUBLANES - (CONV_WIDTH - 1) + t
            acc = acc + cw_ref[t:t + 1, cols] * xbuf[off:off + tm, :]
        xbuf[0:SUBLANES, :] = xbuf[tm:tm + SUBLANES, :]
        qkv_ref[0, :, cols] = acc * _sigmoid(acc)

    def plain(out_ref, off, c0, width):
        out_ref[0, :, c0:c0 + width] = _dot(u, w_ref[:, off + c0:off + c0 + width]).astype(out_ref.dtype)

    others = []
    for out_ref, off, total in ((z_ref, _OFF_Z, DN_WIDTH), (ba_ref, _OFF_BA, LANES),
                                (s5_ref, _OFF_S5, S5_WIDTH), (gab_ref, _OFF_G, gab_ref.shape[-1])):
        for c0 in range(0, total, INPROJ_CB):
            others.append(functools.partial(plain, out_ref, off, c0, min(INPROJ_CB, total - c0)))
    starts = list(range(0, _W_QKV, INPROJ_CB))
    per_conv = len(others) // len(starts)
    for n, c0 in enumerate(starts):
        project(c0)
        mine = others[n * per_conv:(n + 1) * per_conv]
        mine[0]()
        conv_silu(c0)
        for task in mine[1:]:
            task()
    for task in others[len(starts) * per_conv:]:
        task()


def _inproj_call(h, mod, gain, w_cat, conv_w):
    bsz, seq, d = h.shape
    tm = min(INPROJ_TM, seq)
    wg = w_cat.shape[1] - _OFF_G
    widths = (_W_QKV, DN_WIDTH, LANES, S5_WIDTH, wg)
    dtypes = (F32, BF16, F32, F32, BF16)
    return pl.pallas_call(
        functools.partial(_inproj_kernel, tm=tm),
        grid=(bsz, seq // tm),
        in_specs=[pl.BlockSpec((1, tm, d), lambda b, i: (b, i, 0)),
                  pl.BlockSpec((1, N_MOD, d), lambda b, i: (b, 0, 0)),
                  _resident((1, d)), _resident(w_cat.shape), _resident((CONV_WIDTH, _W_QKV))],
        out_specs=[pl.BlockSpec((1, tm, w), lambda b, i: (b, i, 0)) for w in widths],
        out_shape=[jax.ShapeDtypeStruct((bsz, seq, w), t) for w, t in zip(widths, dtypes)],
        scratch_shapes=(_W_QKV // INPROJ_CB) * [pltpu.VMEM((tm + SUBLANES, INPROJ_CB), F32)],
        compiler_params=_cparams(("parallel", "arbitrary")),
        name="inproj",
    )(h, mod, gain.reshape(1, d), w_cat, conv_w)


def _seg_sum(x, seg):
    outs = [_dot(x[:, p * LANES:(p + 1) * LANES].astype(BF16), seg) for p in range(N_PAIRS)]
    return jnp.concatenate(outs, axis=1)


class _PerElement:
    def __init__(self, refs):
        self.refs = refs

    def __getitem__(self, idx):
        if isinstance(idx, tuple):
            return self.refs[idx[0]][idx[1:]]
        return self.refs[idx][...]

    def __setitem__(self, idx, val):
        if isinstance(idx, tuple):
            self.refs[idx[0]][idx[1:]] = val
        else:
            self.refs[idx][...] = val


_DN_N_SCRATCH = 14


def _dn_kernel(qkv_ref, ba_ref, z_ref, hp_ref, gon_ref, o_ref, *scratch, ts, nb):
    w = DN_WIDTH
    hd = DN_HEAD_DIM
    n_chunks = ts // CHUNK
    (st_ref, qn_s, kn_s, kb_s, kbg_s, vb_s, qd_s, kd_s, w_s, attn_s, gc_s, egl_s, o_s, u_s) = [
        _PerElement([scratch[j * _DN_N_SCRATCH + i] for j in range(nb)]) for i in range(_DN_N_SCRATCH)]

    @pl.when(pl.program_id(1) == 0)
    def _():
        for j in range(nb):
            st_ref[j] = jnp.zeros((N_PAIRS, LANES, LANES), F32)

    r128 = lax.broadcasted_iota(jnp.int32, (LANES, LANES), 0)
    c128 = lax.broadcasted_iota(jnp.int32, (LANES, LANES), 1)
    bd_mask = (r128 // hd) == (c128 // hd)
    seg = jnp.where(bd_mask, 1.0, 0.0).astype(BF16)
    er = lax.broadcasted_iota(jnp.int32, (LANES, w), 0)
    ec = lax.broadcasted_iota(jnp.int32, (LANES, w), 1) // hd
    exp_beta = jnp.where(er == ec, 1.0, 0.0).astype(BF16)
    exp_alpha = jnp.where(er - DN_HEADS == ec, 1.0, 0.0).astype(BF16)
    tr = lax.broadcasted_iota(jnp.int32, (ts, ts), 0)
    tc = lax.broadcasted_iota(jnp.int32, (ts, ts), 1)
    same_chunk = (tr // CHUNK) == (tc // CHUNK)
    tri_cum = jnp.where(same_chunk & (tr >= tc), 1.0, 0.0).astype(BF16)
    blk_ones = jnp.where(same_chunk, 1.0, 0.0).astype(BF16)

    ri = lax.broadcasted_iota(jnp.int32, (CHUNK, LANES), 0)
    li = lax.broadcasted_iota(jnp.int32, (CHUNK, LANES), 1)
    jj = li % hd
    causal = ri >= jj
    strict = ri > jj
    eye = ri == jj
    eye_f = jnp.where(eye, 1.0, 0.0)
    in16 = (ri // 16) == (jj // 16)
    in32 = (ri // 32) == (jj // 32)
    in32_only = in32 & jnp.logical_not(in16)
    head0 = li < hd
    zero_b = jnp.zeros((), BF16)

    def bd(y):
        yb = y.astype(BF16)
        return jnp.where(bd_mask, jnp.concatenate([yb, yb], axis=0), zero_b)

    def mm(x, y):
        return _dot(x.astype(BF16), bd(y))

    def tile(j, c, p):
        return (j, slice(c * CHUNK, (c + 1) * CHUNK), slice(p * LANES, (p + 1) * LANES))


    def prepare(j):
        ba = ba_ref[j]
        beta = _sigmoid(ba)
        xa = ba + hp_ref[1:2, :]
        softplus = jnp.maximum(xa, 0.0) + jnp.log1p(jnp.exp(-jnp.abs(xa)))
        log_alpha = -jnp.exp(hp_ref[0:1, :]) * softplus
        la_hi, la_mid, la_lo = _split3(log_alpha)
        g_cum = _dot(tri_cum, la_hi) + _dot(tri_cum, la_mid) + _dot(tri_cum, la_lo)
        g_last = _dot(blk_ones, la_hi) + _dot(blk_ones, la_mid) + _dot(blk_ones, la_lo)
        yield
        beta_f = _dot_sel_r(beta, exp_beta)
        yield
        gc_f = _dot_sel_r(g_cum, exp_alpha)
        yield
        gl_f = _dot_sel_r(g_last, exp_alpha)
        yield
        q = qkv_ref[j, :, 0:w]
        qn = q * (lax.rsqrt(_seg_sum(q * q, seg) + EPS) * (hd ** -0.5))
        qn_s[j] = qn.astype(BF16)
        yield
        k = qkv_ref[j, :, w:2 * w]
        kn = k * lax.rsqrt(_seg_sum(k * k, seg) + EPS)
        kn_s[j] = kn.astype(BF16)
        yield
        eg = jnp.exp(gc_f)
        qd_s[j] = (qn * eg).astype(BF16)
        gc_s[j] = gc_f
        yield
        kb = kn * beta_f
        kb_s[j] = kb.astype(BF16)
        kbg_s[j] = (kb * eg).astype(BF16)
        yield
        kd_s[j] = (kn * jnp.exp(gl_f - gc_f)).astype(BF16)
        egl_s[j] = jnp.exp(gl_f)
        yield
        vb_s[j] = (qkv_ref[j, :, 2 * w:3 * w] * beta_f).astype(BF16)
        yield

    def solve(j):
        items = [tile(j, c, p) for c in range(n_chunks) for p in range(N_PAIRS)]
        kn_c = [kn_s[it] for it in items]
        k2 = [jnp.concatenate([jnp.where(head0, x, zero_b), jnp.where(head0, zero_b, x)], axis=0) for x in kn_c]
        p2 = [_dot_nt(jnp.concatenate([kb_s[it], qn_s[it]], axis=0), y) for it, y in zip(items, k2)]
        yield
        a_mat = []
        for it, pp in zip(items, p2):
            gc_c = gc_s[it]
            gc_row = jnp.sum(jnp.where(eye, gc_c, 0.0), axis=0, keepdims=True)
            decay = jnp.exp(jnp.where(causal, gc_c - gc_row, -1e30))
            a_mat.append(jnp.where(strict, pp[0:CHUNK] * decay, 0.0))
            attn_s[it] = (pp[CHUNK:2 * CHUNK] * decay).astype(BF16)
        yield
        d = [jnp.where(in16, a, 0.0) for a in a_mat]
        d2 = [mm(x, x) for x in d]
        yield
        d4 = [mm(x, x) for x in d2]
        yield
        d8 = [mm(x, x) for x in d4]
        yield
        imd = [eye_f - x for x in d]
        p1 = [x + mm(x, y) for x, y in zip(imd, d2)]
        yield
        ipd4 = [eye_f + x for x in d4]
        pb = [x + mm(x, y) for x, y in zip(ipd4, d8)]
        yield
        t16 = [mm(x, y) for x, y in zip(p1, pb)]
        yield
        f32_ = [mm(t, jnp.where(in32_only, a, 0.0)) for t, a in zip(t16, a_mat)]
        yield
        t32 = [t - mm(f, t) for t, f in zip(t16, f32_)]
        yield
        f64_ = [mm(t, jnp.where(in32, 0.0, a)) for t, a in zip(t32, a_mat)]
        yield
        t64 = [t - mm(f, t) for t, f in zip(t32, f64_)]
        yield
        for it, t in zip(items, t64):
            rhs = jnp.concatenate([bd(vb_s[it]), bd(kbg_s[it])], axis=1)
            uw = _dot(t.astype(BF16), rhs)
            u_s[it] = uw[:, 0:LANES]
            w_s[it] = uw[:, LANES:2 * LANES].astype(BF16)
        yield

    def recur(j):
        state = [st_ref[j, p] for p in range(N_PAIRS)]
        for c in range(n_chunks):
            its = [tile(j, c, p) for p in range(N_PAIRS)]
            rs = [_dot(jnp.concatenate([w_s[it], qd_s[it]], axis=0), s.astype(BF16))
                  for it, s in zip(its, state)]
            yield
            v_new = [u_s[it] - r[0:CHUNK] for it, r in zip(its, rs)]
            upd = [_dot_tn(kd_s[it], v.astype(BF16)) for it, v in zip(its, v_new)]
            o_c = [r[CHUNK:2 * CHUNK] + _dot(attn_s[it], bd(v)) for it, r, v in zip(its, rs, v_new)]
            yield
            state = [s * egl_s[j, c * CHUNK:c * CHUNK + 1, it[2]] + jnp.where(bd_mask, x, 0.0)
                     for it, s, x in zip(its, state, upd)]
            for it, o in zip(its, o_c):
                o_s[it] = o
            yield
        for p in range(N_PAIRS):
            st_ref[j, p] = state[p]

    def finish(j):
        for c in range(n_chunks):
            o = o_s[j, c * CHUNK:(c + 1) * CHUNK, :]
            ms = _seg_sum(o * o, seg) * (1.0 / hd)
            zf = z_ref[j, c * CHUNK:(c + 1) * CHUNK, :].astype(F32)
            out = o * lax.rsqrt(ms + EPS) * gon_ref[...] * (zf * _sigmoid(zf))
            o_ref[j, c * CHUNK:(c + 1) * CHUNK, :] = out.astype(o_ref.dtype)
            yield

    for phase in (prepare, solve, recur, finish):
        _run_interleaved([phase(j) for j in range(nb)])


def _run_interleaved(gens):
    gens = list(gens)
    while gens:
        for g in list(gens):
            try:
                next(g)
            except StopIteration:
                gens.remove(g)


def _dn_call(qkv, ba, z, hp, gon):
    bsz, seq, _ = qkv.shape
    ts = min(DN_TS, seq)
    nb = DN_NB if bsz % DN_NB == 0 else 1
    w = DN_WIDTH
    bf = lambda: pltpu.VMEM((ts, w), BF16)
    ff = lambda: pltpu.VMEM((ts, w), F32)
    tok = lambda width: pl.BlockSpec((nb, ts, width), lambda b, i: (b, i, 0))
    return pl.pallas_call(
        functools.partial(_dn_kernel, ts=ts, nb=nb),
        grid=(bsz // nb, seq // ts),
        in_specs=[tok(3 * w), tok(LANES), tok(w), _resident((SUBLANES, LANES)), _resident((1, w))],
        out_specs=tok(w),
        out_shape=jax.ShapeDtypeStruct((bsz, seq, w), BF16),
        scratch_shapes=nb * [pltpu.VMEM((N_PAIRS, LANES, LANES), F32),
                             bf(), bf(), bf(), bf(), bf(), bf(), bf(), bf(), bf(), ff(), ff(), ff(), ff()],
        compiler_params=_cparams(("parallel", "arbitrary")),
        name="deltanet",
    )(qkv, ba, z, hp, gon)


def _s5_weights(lam_re, lam_im, log_step, b_re, b_im, c_re, c_im):
    hp = lax.Precision.HIGHEST
    g, p = lam_re.shape
    gc = b_re.shape[-1]
    el = S5_L
    lr = jnp.minimum(lam_re.astype(F32), -1e-4)
    li = lam_im.astype(F32)
    step = jnp.exp(log_step.astype(F32))[:, None]
    mag = jnp.exp(lr * step)
    ang = li * step
    lb_re = mag * jnp.cos(ang)
    lb_im = mag * jnp.sin(ang)
    den = lr * lr + li * li
    coef_re = ((lb_re - 1.0) * lr + lb_im * li) / den
    coef_im = (lb_im * lr - (lb_re - 1.0) * li) / den
    b_re = b_re.astype(F32)
    b_im = b_im.astype(F32)
    bb_re = coef_re[..., None] * b_re - coef_im[..., None] * b_im
    bb_im = coef_re[..., None] * b_im + coef_im[..., None] * b_re
    tau = jnp.arange(el + 1, dtype=F32)[:, None, None]
    pw_mag = jnp.exp(tau * (lr * step)[None])
    pw_re = pw_mag * jnp.cos(tau * ang[None])
    pw_im = pw_mag * jnp.sin(tau * ang[None])
    c_re = c_re.astype(F32)
    c_im = c_im.astype(F32)
    cp_re = c_re[None] * pw_re[:, :, None, :] - c_im[None] * pw_im[:, :, None, :]
    cp_im = c_re[None] * pw_im[:, :, None, :] + c_im[None] * pw_re[:, :, None, :]
    kern = (jnp.einsum('tgcp,gpd->tgcd', cp_re[:el], bb_re, precision=hp)
            - jnp.einsum('tgcp,gpd->tgcd', cp_im[:el], bb_im, precision=hp))
    eye_g = jnp.eye(S5_GPB, dtype=F32)
    t_in = jnp.arange(el)[:, None]
    t_out = jnp.arange(el)[None, :]
    lag = jnp.clip(t_out - t_in, 0, el - 1)
    toe = jnp.where((t_out >= t_in)[:, :, None, None, None], kern[lag], 0.0)
    toe = toe.reshape(el, el, S5_GB, S5_GPB, gc, gc)
    w_t = jnp.einsum('tubgoi,gh->btgiuho', toe, eye_g).reshape(S5_GB, el * LANES, el * LANES)
    tau_rev = (el - 1.0) - tau[:el]
    rev_mag = jnp.exp(tau_rev * (lr * step)[None])
    rev_re = rev_mag * jnp.cos(tau_rev * ang[None])
    rev_im = rev_mag * jnp.sin(tau_rev * ang[None])
    in_re = rev_re[..., None] * bb_re[None] - rev_im[..., None] * bb_im[None]
    in_im = rev_re[..., None] * bb_im[None] + rev_im[..., None] * bb_re[None]
    w_in = jnp.stack([in_re, in_im], axis=0).reshape(2, el, S5_GB, S5_GPB, p, gc)
    w_in = jnp.einsum('rtbgpi,gh->btgirhp', w_in, eye_g).reshape(S5_GB, el * LANES, 2 * S5_SW)
    w_o = jnp.stack([cp_re[1:], -cp_im[1:]], axis=0).reshape(2, el, S5_GB, S5_GPB, gc, p)
    w_o = jnp.einsum('rubgop,gh->brgpuho', w_o, eye_g).reshape(S5_GB, 2 * S5_SW, el * LANES)
    lam = jnp.stack([pw_re[el], pw_im[el]], axis=0).reshape(2, S5_GB, S5_SW).transpose(1, 0, 2)
    w_a = jnp.concatenate([w_t, w_in], axis=2)
    return w_a.astype(BF16), w_o.astype(BF16), lam


def _s5_kernel(u_ref, wa_ref, wo_ref, lam_ref, d_ref, y_ref, s_s, xp_s, *, nrow):
    el = S5_L
    sw = S5_SW
    xs = [u_ref[0, pl.ds(t, nrow, stride=el), :] for t in range(el)]
    xcat = jnp.concatenate([x.astype(BF16) for x in xs], axis=1)
    r = _dot(xcat, wa_ref[0])
    s_s[...] = r[:, el * LANES:]
    lam_r = lam_ref[0, 0:1, :]
    lam_i = lam_ref[0, 1:2, :]

    def step(n, carry):
        x_r, x_i = carry
        xp_s[pl.ds(n, 1), 0:sw] = x_r
        xp_s[pl.ds(n, 1), sw:2 * sw] = x_i
        s_r = s_s[pl.ds(n, 1), 0:sw]
        s_i = s_s[pl.ds(n, 1), sw:2 * sw]
        return (lam_r * x_r - lam_i * x_i + s_r, lam_r * x_i + lam_i * x_r + s_i)

    zero = jnp.zeros((1, sw), F32)
    lax.fori_loop(0, nrow, step, (zero, zero))
    y = r[:, 0:el * LANES] + _dot(xp_s[...].astype(BF16), wo_ref[0])
    dsk = d_ref[0]
    for t in range(el):
        y_ref[0, pl.ds(t, nrow, stride=el), :] = y[:, t * LANES:(t + 1) * LANES] + dsk * xs[t]


def _s5_call(u, w_a, w_o, lam, d_skip):
    bsz, seq, width = u.shape
    nrow = seq // S5_L
    return pl.pallas_call(
        functools.partial(_s5_kernel, nrow=nrow),
        grid=(S5_GB, bsz),
        in_specs=[pl.BlockSpec((1, seq, LANES), lambda g, b: (b, 0, g)),
                  pl.BlockSpec((1,) + w_a.shape[1:], lambda g, b: (g, 0, 0)),
                  pl.BlockSpec((1,) + w_o.shape[1:], lambda g, b: (g, 0, 0)),
                  pl.BlockSpec((1, 2, S5_SW), lambda g, b: (g, 0, 0)),
                  pl.BlockSpec((1, 1, LANES), lambda g, b: (g, 0, 0))],
        out_specs=pl.BlockSpec((1, seq, LANES), lambda g, b: (b, 0, g)),
        out_shape=jax.ShapeDtypeStruct((bsz, seq, width), F32),
        scratch_shapes=[pltpu.VMEM((nrow, 2 * S5_SW), F32), pltpu.VMEM((nrow, 2 * S5_SW), F32)],
        compiler_params=_cparams(("arbitrary", "arbitrary")),
        name="s5",
    )(u, w_a, w_o, lam, d_skip.reshape(S5_GB, 1, LANES))


def _gelu_tanh(x):
    return 0.5 * x * (1.0 + jnp.tanh(math.sqrt(2.0 / math.pi) * (x + 0.044715 * (x * x * x))))


def _merge_kernel(h_ref, oa_ref, ys_ref, gab_ref, mod_ref, wglu_ref, bglu_ref, wpa_ref, wpb_ref, wout_ref, o_ref):
    d = h_ref.shape[-1]
    yb = _gelu_tanh(ys_ref[0])
    yb = yb * jax.nn.sigmoid(_dot(yb.astype(BF16), wglu_ref[...]) + bglu_ref[...])
    y_b = _dot(yb.astype(BF16), wpb_ref[...])
    y_a = _dot(oa_ref[0], wpa_ref[...])
    gab = gab_ref[0].astype(F32)
    merged = jax.nn.sigmoid(gab[:, 0:d]) * y_a + jax.nn.sigmoid(gab[:, d:2 * d]) * y_b
    o_ref[0] = h_ref[0] + mod_ref[0, 5:6, :] * _dot(merged.astype(BF16), wout_ref[...])


def _merge_call(h, o_a, y_s, gab, mod, w_glu, b_glu, w_pa, w_pb, w_out):
    bsz, seq, d = h.shape
    tm = min(MERGE_TM, seq)
    tok = lambda w: pl.BlockSpec((1, tm, w), lambda b, i: (b, i, 0))
    return pl.pallas_call(
        _merge_kernel,
        grid=(bsz, seq // tm),
        in_specs=[tok(d), tok(o_a.shape[-1]), tok(y_s.shape[-1]), tok(gab.shape[-1]),
                  pl.BlockSpec((1, N_MOD, d), lambda b, i: (b, 0, 0)),
                  _resident(w_glu.shape), _resident((1, b_glu.shape[-1])), _resident(w_pa.shape),
                  _resident(w_pb.shape), _resident(w_out.shape)],
        out_specs=tok(d),
        out_shape=jax.ShapeDtypeStruct((bsz, seq, d), F32),
        compiler_params=_cparams(("parallel", "parallel")),
        name="merge",
    )(h, o_a, y_s, gab, mod, w_glu, b_glu.reshape(1, -1), w_pa, w_pb, w_out)


def _regroup_w_in(w_in):
    w = DN_WIDTH
    ba = w_in[:, 4 * w:4 * w + 2 * DN_HEADS]
    ba = jnp.pad(ba, ((0, 0), (0, LANES - 2 * DN_HEADS)))
    s5_off = 4 * w + 2 * DN_HEADS
    return jnp.concatenate([w_in[:, 0:3 * w], w_in[:, 3 * w:4 * w], ba,
                            w_in[:, s5_off:s5_off + S5_WIDTH], w_in[:, s5_off + S5_WIDTH:]], axis=1).astype(BF16)


def kernel(x, c, w_ada, b_ada, g_ffn1, w1_ffn1, w3_ffn1, w2_ffn1, g_mix, w_in, conv_qkv, a_log, dt_bias, g_onorm, lam_re, lam_im, log_step, b_re, b_im, c_re, c_im, d_skip, w_glu, b_glu, w_proj_a, w_proj_b, w_out, g_ffn2, w1_ffn2, w3_ffn2, w2_ffn2, g_final):
    bsz, seq, d = x.shape
    depth = w_ada.shape[0]
    h = x
    for l in range(depth):
        last = l == depth - 1
        mod = _mod_call(c, w_ada[l], b_ada[l]).reshape(bsz, N_MOD, d)
        h = _ffn_call(h, mod, g_ffn1[l], w1_ffn1[l].astype(BF16), w3_ffn1[l].astype(BF16),
                      w2_ffn1[l].astype(BF16), g_final, row=0, final=False)
        qkv, z, ba, s5_in, gab = _inproj_call(h, mod, g_mix[l], _regroup_w_in(w_in[l]), conv_qkv[l])
        hp = jnp.zeros((SUBLANES, LANES), F32)
        hp = hp.at[0, DN_HEADS:2 * DN_HEADS].set(a_log[l]).at[1, DN_HEADS:2 * DN_HEADS].set(dt_bias[l])
        gon = jnp.tile(g_onorm[l], DN_HEADS).reshape(1, DN_WIDTH)
        o_a = _dn_call(qkv, ba, z, hp, gon)
        w_a, w_o, lam = _s5_weights(lam_re[l], lam_im[l], log_step[l], b_re[l], b_im[l], c_re[l], c_im[l])
        y_s = _s5_call(s5_in, w_a, w_o, lam, d_skip[l])
        h = _merge_call(h, o_a, y_s, gab, mod, w_glu[l].astype(BF16), b_glu[l], w_proj_a[l].astype(BF16),
                        w_proj_b[l].astype(BF16), w_out[l].astype(BF16))
        h = _ffn_call(h, mod, g_ffn2[l], w1_ffn2[l].astype(BF16), w3_ffn2[l].astype(BF16),
                      w2_ffn2[l].astype(BF16), g_final, row=6, final=last)
    if depth == 0:
        raise ValueError("depth must be positive")
    return h
```

```python
import functools
import math

import jax
import jax.numpy as jnp
from jax import lax
from jax.experimental import pallas as pl
from jax.experimental.pallas import tpu as pltpu

F32 = jnp.float32
BF16 = jnp.bfloat16
EPS = 1e-6

LANES = 128
SUBLANES = 8
VMEM_LIMIT_BYTES = 56 * 1024 * 1024

N_MOD = 9
DN_HEADS = 8
DN_HEAD_DIM = 64
DN_WIDTH = DN_HEADS * DN_HEAD_DIM
CONV_WIDTH = 4
CHUNK = 64
N_PAIRS = DN_WIDTH // LANES
S5_GROUP_CH = 16
S5_STATE = 64
S5_WIDTH = 512
S5_GB = S5_WIDTH // LANES
S5_GPB = LANES // S5_GROUP_CH
S5_L = 8
S5_SW = S5_GPB * S5_STATE
S5_NB = 4

FFN_TM = 512
INPROJ_TM = 512
INPROJ_CB = 256
DN_TS = 256
DN_NB = 4
MERGE_TM = 512


def _dot(a, b):
    return jnp.dot(a, b, preferred_element_type=F32)


def _dot_nt(a, b):
    return lax.dot_general(a, b, (((1,), (1,)), ((), ())), preferred_element_type=F32)


def _dot_tn(a, b):
    return lax.dot_general(a, b, (((0,), (0,)), ((), ())), preferred_element_type=F32)


def _split3(x):
    hi = x.astype(BF16)
    r = x - hi.astype(F32)
    mid = r.astype(BF16)
    lo = (r - mid.astype(F32)).astype(BF16)
    return hi, mid, lo


def _dot_sel_r(x, sel):
    hi = x.astype(BF16)
    lo = (x - hi.astype(F32)).astype(BF16)
    return _dot(hi, sel) + _dot(lo, sel)


def _silu(x):
    return x * jax.nn.sigmoid(x)


def _sigmoid(x):
    return 0.5 * jnp.tanh(0.5 * x) + 0.5


def _norm_mod(x, gain, shift, scale):
    ms = jnp.mean(x * x, axis=-1, keepdims=True)
    y = x * lax.rsqrt(ms + EPS) * gain
    return y * (1.0 + scale) + shift


def _cparams(sem):
    return pltpu.CompilerParams(dimension_semantics=sem, vmem_limit_bytes=VMEM_LIMIT_BYTES)


def _resident(shape):
    nd = len(shape)
    return pl.BlockSpec(shape, lambda *_: (0,) * nd, pipeline_mode=pl.Buffered(1))


def _mod_kernel(c_ref, w_ref, b_ref, o_ref):
    a = _silu(c_ref[...])
    o_ref[...] = jnp.dot(a, w_ref[...], preferred_element_type=F32,
                         precision=lax.Precision.HIGHEST) + b_ref[...]


def _mod_call(c, w_ada, b_ada):
    bsz, d = c.shape
    n = w_ada.shape[1]
    return pl.pallas_call(
        _mod_kernel,
        grid=(n // d,),
        in_specs=[pl.BlockSpec((bsz, d), lambda j: (0, 0)),
                  pl.BlockSpec((d, d), lambda j: (0, j)),
                  pl.BlockSpec((1, d), lambda j: (0, j))],
        out_specs=pl.BlockSpec((bsz, d), lambda j: (0, j)),
        out_shape=jax.ShapeDtypeStruct((bsz, n), F32),
        compiler_params=_cparams(("arbitrary",)),
        name="mod",
    )(c, w_ada, b_ada.reshape(1, n))


def _ffn_kernel(x_ref, mod_ref, g_ref, w1_ref, w3_ref, w2_ref, gf_ref, o_ref, *, row, final):
    x = x_ref[0]
    shift = mod_ref[0, row:row + 1, :]
    scale = mod_ref[0, row + 1:row + 2, :]
    gate = mod_ref[0, row + 2:row + 3, :]
    u = _norm_mod(x, g_ref[...], shift, scale).astype(BF16)
    a = (_silu(_dot(u, w1_ref[...])) * _dot(u, w3_ref[...])).astype(BF16)
    h = x + 0.5 * gate * _dot(a, w2_ref[...])
    if final:
        ms = jnp.mean(h * h, axis=-1, keepdims=True)
        h = h * lax.rsqrt(ms + EPS) * gf_ref[...]
    o_ref[0] = h


def _ffn_call(x, mod, gain, w1, w3, w2, g_final, *, row, final):
    bsz, seq, d = x.shape
    f = w1.shape[1]
    tm = min(FFN_TM, seq)
    return pl.pallas_call(
        functools.partial(_ffn_kernel, row=row, final=final),
        grid=(bsz, seq // tm),
        in_specs=[pl.BlockSpec((1, tm, d), lambda b, i: (b, i, 0)),
                  pl.BlockSpec((1, N_MOD, d), lambda b, i: (b, 0, 0)),
                  _resident((1, d)), _resident((d, f)), _resident((d, f)), _resident((f, d)),
                  _resident((1, d))],
        out_specs=pl.BlockSpec((1, tm, d), lambda b, i: (b, i, 0)),
        out_shape=jax.ShapeDtypeStruct((bsz, seq, d), F32),
        compiler_params=_cparams(("parallel", "parallel")),
        name="ffn_final" if final else "ffn",
    )(x, mod, gain.reshape(1, d), w1, w3, w2, g_final.reshape(1, d))


_W_QKV = 3 * DN_WIDTH
_OFF_Z = _W_QKV
_OFF_BA = _OFF_Z + DN_WIDTH
_OFF_S5 = _OFF_BA + LANES
_OFF_G = _OFF_S5 + S5_WIDTH


def _inproj_kernel(h_ref, mod_ref, g_ref, w_ref, cw_ref, qkv_ref, z_ref, ba_ref, s5_ref, gab_ref, *xbufs, tm):
    @pl.when(pl.program_id(1) == 0)
    def _():
        for xbuf in xbufs:
            xbuf[0:SUBLANES, :] = jnp.zeros((SUBLANES, INPROJ_CB), F32)

    u = _norm_mod(h_ref[0], g_ref[...], mod_ref[0, 3:4, :], mod_ref[0, 4:5, :]).astype(BF16)

    def project(c0):
        xbuf = xbufs[c0 // INPROJ_CB]
        xbuf[SUBLANES:SUBLANES + tm, :] = _dot(u, w_ref[:, c0:c0 + INPROJ_CB])

    def conv_silu(c0, l0):
        xbuf = xbufs[c0 // INPROJ_CB]
        lanes = slice(l0, l0 + LANES)
        cols = slice(c0 + l0, c0 + l0 + LANES)
        acc = cw_ref[CONV_WIDTH - 1:CONV_WIDTH, cols] * xbuf[SUBLANES:SUBLANES + tm, lanes]
        for t in range(CONV_WIDTH - 1):
            off = SUBLANES - (CONV_WIDTH - 1) + t
            acc = acc + cw_ref[t:t + 1, cols] * xbuf[off:off + tm, lanes]
        xbuf[0:SUBLANES, lanes] = xbuf[tm:tm + SUBLANES, lanes]
        qkv_ref[0, :, cols] = acc * _sigmoid(acc)

    def plain(out_ref, off, c0, width):
        out_ref[0, :, c0:c0 + width] = _dot(u, w_ref[:, off + c0:off + c0 + width]).astype(out_ref.dtype)

    others = []
    for out_ref, off, total in ((z_ref, _OFF_Z, DN_WIDTH), (ba_ref, _OFF_BA, LANES),
                                (s5_ref, _OFF_S5, S5_WIDTH), (gab_ref, _OFF_G, gab_ref.shape[-1])):
        for c0 in range(0, total, INPROJ_CB):
            others.append(functools.partial(plain, out_ref, off, c0, min(INPROJ_CB, total - c0)))
    convs = [functools.partial(conv_silu, c0, l0)
             for c0 in range(0, _W_QKV, INPROJ_CB) for l0 in range(0, INPROJ_CB, LANES)]
    per_block = INPROJ_CB // LANES
    for n, c0 in enumerate(range(0, _W_QKV, INPROJ_CB)):
        project(c0)
        for piece in convs[n * per_block:(n + 1) * per_block]:
            if others:
                others.pop(0)()
            piece()
    for task in others:
        task()


def _inproj_call(h, mod, gain, w_cat, conv_w):
    bsz, seq, d = h.shape
    tm = min(INPROJ_TM, seq)
    wg = w_cat.shape[1] - _OFF_G
    widths = (_W_QKV, DN_WIDTH, LANES, S5_WIDTH, wg)
    dtypes = (F32, BF16, F32, F32, BF16)
    return pl.pallas_call(
        functools.partial(_inproj_kernel, tm=tm),
        grid=(bsz, seq // tm),
        in_specs=[pl.BlockSpec((1, tm, d), lambda b, i: (b, i, 0)),
                  pl.BlockSpec((1, N_MOD, d), lambda b, i: (b, 0, 0)),
                  _resident((1, d)), _resident(w_cat.shape), _resident((CONV_WIDTH, _W_QKV))],
        out_specs=[pl.BlockSpec((1, tm, w), lambda b, i: (b, i, 0)) for w in widths],
        out_shape=[jax.ShapeDtypeStruct((bsz, seq, w), t) for w, t in zip(widths, dtypes)],
        scratch_shapes=(_W_QKV // INPROJ_CB) * [pltpu.VMEM((tm + SUBLANES, INPROJ_CB), F32)],
        compiler_params=_cparams(("parallel", "arbitrary")),
        name="inproj",
    )(h, mod, gain.reshape(1, d), w_cat, conv_w)


def _seg_sum(x, seg):
    outs = [_dot(x[:, p * LANES:(p + 1) * LANES].astype(BF16), seg) for p in range(N_PAIRS)]
    return jnp.concatenate(outs, axis=1)


class _PerElement:
    def __init__(self, refs):
        self.refs = refs

    def __getitem__(self, idx):
        if isinstance(idx, tuple):
            return self.refs[idx[0]][idx[1:]]
        return self.refs[idx][...]

    def __setitem__(self, idx, val):
        if isinstance(idx, tuple):
            self.refs[idx[0]][idx[1:]] = val
        else:
            self.refs[idx][...] = val


_DN_N_SCRATCH = 14


def _dn_kernel(qkv_ref, ba_ref, z_ref, hp_ref, gon_ref, o_ref, *scratch, ts, nb):
    w = DN_WIDTH
    hd = DN_HEAD_DIM
    n_chunks = ts // CHUNK
    (st_ref, qn_s, kn_s, kb_s, kbg_s, vb_s, qd_s, kd_s, w_s, attn_s, gc_s, egl_s, o_s, u_s) = [
        _PerElement([scratch[j * _DN_N_SCRATCH + i] for j in range(nb)]) for i in range(_DN_N_SCRATCH)]

    @pl.when(pl.program_id(1) == 0)
    def _():
        for j in range(nb):
            st_ref[j] = jnp.zeros((N_PAIRS, LANES, LANES), F32)

    r128 = lax.broadcasted_iota(jnp.int32, (LANES, LANES), 0)
    c128 = lax.broadcasted_iota(jnp.int32, (LANES, LANES), 1)
    bd_mask = (r128 // hd) == (c128 // hd)
    seg = jnp.where(bd_mask, 1.0, 0.0).astype(BF16)
    er = lax.broadcasted_iota(jnp.int32, (LANES, w), 0)
    ec = lax.broadcasted_iota(jnp.int32, (LANES, w), 1) // hd
    exp_beta = jnp.where(er == ec, 1.0, 0.0).astype(BF16)
    exp_alpha = jnp.where(er - DN_HEADS == ec, 1.0, 0.0).astype(BF16)
    tr = lax.broadcasted_iota(jnp.int32, (ts, ts), 0)
    tc = lax.broadcasted_iota(jnp.int32, (ts, ts), 1)
    same_chunk = (tr // CHUNK) == (tc // CHUNK)
    tri_cum = jnp.where(same_chunk & (tr >= tc), 1.0, 0.0).astype(BF16)
    blk_ones = jnp.where(same_chunk, 1.0, 0.0).astype(BF16)

    ri = lax.broadcasted_iota(jnp.int32, (CHUNK, LANES), 0)
    li = lax.broadcasted_iota(jnp.int32, (CHUNK, LANES), 1)
    jj = li % hd
    causal = ri >= jj
    strict = ri > jj
    eye = ri == jj
    eye_f = jnp.where(eye, 1.0, 0.0)
    in16 = (ri // 16) == (jj // 16)
    in32 = (ri // 32) == (jj // 32)
    in32_only = in32 & jnp.logical_not(in16)
    head0 = li < hd
    zero_b = jnp.zeros((), BF16)

    def bd(y):
        yb = y.astype(BF16)
        return jnp.where(bd_mask, jnp.concatenate([yb, yb], axis=0), zero_b)

    def mm(x, y):
        return _dot(x.astype(BF16), bd(y))

    def tile(j, c, p):
        return (j, slice(c * CHUNK, (c + 1) * CHUNK), slice(p * LANES, (p + 1) * LANES))


    def prepare(j):
        ba = ba_ref[j]
        beta = _sigmoid(ba)
        xa = ba + hp_ref[1:2, :]
        softplus = jnp.maximum(xa, 0.0) + jnp.log1p(jnp.exp(-jnp.abs(xa)))
        log_alpha = -jnp.exp(hp_ref[0:1, :]) * softplus
        la_hi, la_mid, la_lo = _split3(log_alpha)
        g_cum = _dot(tri_cum, la_hi) + _dot(tri_cum, la_mid) + _dot(tri_cum, la_lo)
        g_last = _dot(blk_ones, la_hi) + _dot(blk_ones, la_mid) + _dot(blk_ones, la_lo)
        yield
        beta_f = _dot_sel_r(beta, exp_beta)
        yield
        gc_f = _dot_sel_r(g_cum, exp_alpha)
        yield
        gl_f = _dot_sel_r(g_last, exp_alpha)
        yield
        q = qkv_ref[j, :, 0:w]
        qn = q * (lax.rsqrt(_seg_sum(q * q, seg) + EPS) * (hd ** -0.5))
        qn_s[j] = qn.astype(BF16)
        yield
        k = qkv_ref[j, :, w:2 * w]
        kn = k * lax.rsqrt(_seg_sum(k * k, seg) + EPS)
        kn_s[j] = kn.astype(BF16)
        yield
        eg = jnp.exp(gc_f)
        qd_s[j] = (qn * eg).astype(BF16)
        gc_s[j] = gc_f
        yield
        kb = kn * beta_f
        kb_s[j] = kb.astype(BF16)
        kbg_s[j] = (kb * eg).astype(BF16)
        yield
        kd_s[j] = (kn * jnp.exp(gl_f - gc_f)).astype(BF16)
        egl_s[j] = jnp.exp(gl_f)
        yield
        vb_s[j] = (qkv_ref[j, :, 2 * w:3 * w] * beta_f).astype(BF16)
        yield

    def solve(j):
        items = [tile(j, c, p) for c in range(n_chunks) for p in range(N_PAIRS)]
        kn_c = [kn_s[it] for it in items]
        k2 = [jnp.concatenate([jnp.where(head0, x, zero_b), jnp.where(head0, zero_b, x)], axis=0) for x in kn_c]
        p2 = [_dot_nt(jnp.concatenate([kb_s[it], qn_s[it]], axis=0), y) for it, y in zip(items, k2)]
        yield
        a_mat = []
        for it, pp in zip(items, p2):
            gc_c = gc_s[it]
            gc_row = jnp.sum(jnp.where(eye, gc_c, 0.0), axis=0, keepdims=True)
            decay = jnp.exp(jnp.where(causal, gc_c - gc_row, -1e30))
            a_mat.append(jnp.where(strict, pp[0:CHUNK] * decay, 0.0))
            attn_s[it] = (pp[CHUNK:2 * CHUNK] * decay).astype(BF16)
        yield
        d = [jnp.where(in16, a, 0.0) for a in a_mat]
        d2 = [mm(x, x) for x in d]
        yield
        d4 = [mm(x, x) for x in d2]
        yield
        d8 = [mm(x, x) for x in d4]
        yield
        imd = [eye_f - x for x in d]
        p1 = [x + mm(x, y) for x, y in zip(imd, d2)]
        yield
        ipd4 = [eye_f + x for x in d4]
        pb = [x + mm(x, y) for x, y in zip(ipd4, d8)]
        yield
        t16 = [mm(x, y) for x, y in zip(p1, pb)]
        yield
        f32_ = [mm(t, jnp.where(in32_only, a, 0.0)) for t, a in zip(t16, a_mat)]
        yield
        t32 = [t - mm(f, t) for t, f in zip(t16, f32_)]
        yield
        f64_ = [mm(t, jnp.where(in32, 0.0, a)) for t, a in zip(t32, a_mat)]
        yield
        t64 = [t - mm(f, t) for t, f in zip(t32, f64_)]
        yield
        for it, t in zip(items, t64):
            rhs = jnp.concatenate([bd(vb_s[it]), bd(kbg_s[it])], axis=1)
            uw = _dot(t.astype(BF16), rhs)
            u_s[it] = uw[:, 0:LANES]
            w_s[it] = uw[:, LANES:2 * LANES].astype(BF16)
        yield

    def recur(j):
        state = [st_ref[j, p] for p in range(N_PAIRS)]
        for c in range(n_chunks):
            its = [tile(j, c, p) for p in range(N_PAIRS)]
            rs = [_dot(jnp.concatenate([w_s[it], qd_s[it]], axis=0), s.astype(BF16))
                  for it, s in zip(its, state)]
            yield
            v_new = [u_s[it] - r[0:CHUNK] for it, r in zip(its, rs)]
            upd = [_dot_tn(kd_s[it], v.astype(BF16)) for it, v in zip(its, v_new)]
            o_c = [r[CHUNK:2 * CHUNK] + _dot(attn_s[it], bd(v)) for it, r, v in zip(its, rs, v_new)]
            yield
            state = [s * egl_s[j, c * CHUNK:c * CHUNK + 1, it[2]] + jnp.where(bd_mask, x, 0.0)
                     for it, s, x in zip(its, state, upd)]
            for it, o in zip(its, o_c):
                o_s[it] = o
            yield
        for p in range(N_PAIRS):
            st_ref[j, p] = state[p]

    def finish(j):
        for c in range(n_chunks):
            o = o_s[j, c * CHUNK:(c + 1) * CHUNK, :]
            ms = _seg_sum(o * o, seg) * (1.0 / hd)
            zf = z_ref[j, c * CHUNK:(c + 1) * CHUNK, :].astype(F32)
            out = o * lax.rsqrt(ms + EPS) * gon_ref[...] * (zf * _sigmoid(zf))
            o_ref[j, c * CHUNK:(c + 1) * CHUNK, :] = out.astype(o_ref.dtype)
            yield

    for phase in (prepare, solve, recur, finish):
        _run_interleaved([phase(j) for j in range(nb)])


def _run_interleaved(gens):
    gens = list(gens)
    while gens:
        for g in list(gens):
            try:
                next(g)
            except StopIteration:
                gens.remove(g)


def _dn_call(qkv, ba, z, hp, gon):
    bsz, seq, _ = qkv.shape
    ts = min(DN_TS, seq)
    nb = DN_NB if bsz % DN_NB == 0 else 1
    w = DN_WIDTH
    bf = lambda: pltpu.VMEM((ts, w), BF16)
    ff = lambda: pltpu.VMEM((ts, w), F32)
    tok = lambda width: pl.BlockSpec((nb, ts, width), lambda b, i: (b, i, 0))
    return pl.pallas_call(
        functools.partial(_dn_kernel, ts=ts, nb=nb),
        grid=(bsz // nb, seq // ts),
        in_specs=[tok(3 * w), tok(LANES), tok(w), _resident((SUBLANES, LANES)), _resident((1, w))],
        out_specs=tok(w),
        out_shape=jax.ShapeDtypeStruct((bsz, seq, w), BF16),
        scratch_shapes=nb * [pltpu.VMEM((N_PAIRS, LANES, LANES), F32),
                             bf(), bf(), bf(), bf(), bf(), bf(), bf(), bf(), bf(), ff(), ff(), ff(), ff()],
        compiler_params=_cparams(("parallel", "arbitrary")),
        name="deltanet",
    )(qkv, ba, z, hp, gon)


def _s5_prep_kernel(lre_ref, lim_ref, lstep_ref, btr_ref, bti_ref, cxr_ref, cxi_ref, wa_ref, wot_ref, lam_ref):
    el = S5_L
    sw = S5_SW
    lr = jnp.minimum(lre_ref[0], -1e-4)
    li = lim_ref[0]
    step = jnp.exp(lstep_ref[0])
    zr = lr * step
    ang = li * step
    mag = jnp.exp(zr)
    lb_re = mag * jnp.cos(ang)
    lb_im = mag * jnp.sin(ang)
    den = lr * lr + li * li
    coef_re = ((lb_re - 1.0) * lr + lb_im * li) / den
    coef_im = (lb_im * lr - (lb_re - 1.0) * li) / den

    def power(tau):
        m = jnp.exp(tau * zr)
        return m * jnp.cos(tau * ang), m * jnp.sin(tau * ang)

    rg = lax.broadcasted_iota(jnp.int32, (LANES, sw), 0) // S5_GROUP_CH
    cg = lax.broadcasted_iota(jnp.int32, (LANES, sw), 1) // S5_STATE
    same = rg == cg

    def spread(ref):
        return jnp.where(same, jnp.concatenate([ref[0]] * S5_GPB, axis=0), 0.0)

    bt_re = spread(btr_ref)
    bt_im = spread(bti_ref)
    bb_re = coef_re * bt_re - coef_im * bt_im
    bb_im = coef_re * bt_im + coef_im * bt_re
    cx_re = spread(cxr_ref)
    cx_im = spread(cxi_ref)

    for t in range(el):
        pr, pi = power(float(el - 1 - t))
        rows = slice(t * LANES, (t + 1) * LANES)
        wa_ref[0, rows, el * LANES:el * LANES + sw] = (pr * bb_re - pi * bb_im).astype(wa_ref.dtype)
        wa_ref[0, rows, el * LANES + sw:el * LANES + 2 * sw] = (pr * bb_im + pi * bb_re).astype(wa_ref.dtype)

    cl = []
    for tau in range(el + 1):
        pr, pi = power(float(tau))
        cl.append((cx_re * pr - cx_im * pi, cx_re * pi + cx_im * pr))
    for t2 in range(el):
        q_re, q_im = cl[t2 + 1]
        rows = slice(t2 * LANES, (t2 + 1) * LANES)
        wot_ref[0, rows, 0:sw] = q_re.astype(wot_ref.dtype)
        wot_ref[0, rows, sw:2 * sw] = (-q_im).astype(wot_ref.dtype)

    def contract(a, b):
        return lax.dot_general(a, b, (((1,), (1,)), ((), ())), preferred_element_type=F32,
                               precision=lax.Precision.HIGHEST)

    kern = [(contract(bb_re, cl[tau][0]) - contract(bb_im, cl[tau][1])).astype(wa_ref.dtype) for tau in range(el)]
    zero = jnp.zeros((LANES, LANES), wa_ref.dtype)
    for t in range(el):
        for t2 in range(el):
            wa_ref[0, t * LANES:(t + 1) * LANES, t2 * LANES:(t2 + 1) * LANES] = kern[t2 - t] if t2 >= t else zero

    pr, pi = power(float(el))
    lam_ref[0, 0:1, :] = pr
    lam_ref[0, 1:2, :] = pi


def _s5_prep_call(lam_re, lam_im, log_step, b_re, b_im, c_re, c_im):
    g, p = lam_re.shape
    gc = b_re.shape[-1]
    el = S5_L
    row = lambda x: x.astype(F32).reshape(S5_GB, 1, S5_SW)
    bt = lambda x: x.astype(F32).reshape(S5_GB, S5_GPB, p, gc).transpose(0, 3, 1, 2).reshape(S5_GB, gc, S5_SW)
    cx = lambda x: x.astype(F32).reshape(S5_GB, S5_GPB, gc, p).transpose(0, 2, 1, 3).reshape(S5_GB, gc, S5_SW)
    vec = pl.BlockSpec((1, 1, S5_SW), lambda i: (i, 0, 0))
    mat = pl.BlockSpec((1, gc, S5_SW), lambda i: (i, 0, 0))
    wa_cols = el * LANES + 2 * S5_SW
    return pl.pallas_call(
        _s5_prep_kernel,
        grid=(S5_GB,),
        in_specs=[vec, vec, vec, mat, mat, mat, mat],
        out_specs=[pl.BlockSpec((1, el * LANES, wa_cols), lambda i: (i, 0, 0)),
                   pl.BlockSpec((1, el * LANES, 2 * S5_SW), lambda i: (i, 0, 0)),
                   pl.BlockSpec((1, 2, S5_SW), lambda i: (i, 0, 0))],
        out_shape=[jax.ShapeDtypeStruct((S5_GB, el * LANES, wa_cols), BF16),
                   jax.ShapeDtypeStruct((S5_GB, el * LANES, 2 * S5_SW), BF16),
                   jax.ShapeDtypeStruct((S5_GB, 2, S5_SW), F32)],
        compiler_params=_cparams(("arbitrary",)),
        name="s5_prep",
    )(row(lam_re), row(lam_im), row(jnp.repeat(log_step, p)), bt(b_re), bt(b_im), cx(c_re), cx(c_im))


def _s5_kernel(u_ref, wa_ref, wo_ref, lam_ref, d_ref, y_ref, s_s, xp_s, *, nrow, nb):
    el = S5_L
    sw = S5_SW
    xs = [u_ref[:, pl.ds(t, nrow, stride=el), :].reshape(nb * nrow, LANES) for t in range(el)]
    xcat = jnp.concatenate([x.astype(BF16) for x in xs], axis=1)
    r = _dot(xcat, wa_ref[0])
    s_s[...] = r[:, el * LANES:]
    lam_r = lam_ref[0, 0:1, :]
    lam_i = lam_ref[0, 1:2, :]

    def step(n, carry):
        new = []
        for j in range(nb):
            x_r, x_i = carry[2 * j], carry[2 * j + 1]
            row = pl.ds(j * nrow + n, 1)
            xp_s[row, 0:sw] = x_r
            xp_s[row, sw:2 * sw] = x_i
            new.append(lam_r * x_r - lam_i * x_i + s_s[row, 0:sw])
            new.append(lam_r * x_i + lam_i * x_r + s_s[row, sw:2 * sw])
        return tuple(new)

    zero = jnp.zeros((1, sw), F32)
    lax.fori_loop(0, nrow, step, (zero,) * (2 * nb), unroll=4)
    y = r[:, 0:el * LANES] + _dot_nt(xp_s[...].astype(BF16), wo_ref[0])
    dsk = d_ref[0]
    for t in range(el):
        y_t = y[:, t * LANES:(t + 1) * LANES] + dsk * xs[t]
        y_ref[:, pl.ds(t, nrow, stride=el), :] = y_t.reshape(nb, nrow, LANES)


def _s5_call(u, w_a, w_o, lam, d_skip):
    bsz, seq, width = u.shape
    nrow = seq // S5_L
    nb = S5_NB if bsz % S5_NB == 0 else 1
    return pl.pallas_call(
        functools.partial(_s5_kernel, nrow=nrow, nb=nb),
        grid=(S5_GB, bsz // nb),
        in_specs=[pl.BlockSpec((nb, seq, LANES), lambda g, b: (b, 0, g)),
                  pl.BlockSpec((1,) + w_a.shape[1:], lambda g, b: (g, 0, 0)),
                  pl.BlockSpec((1,) + w_o.shape[1:], lambda g, b: (g, 0, 0)),
                  pl.BlockSpec((1, 2, S5_SW), lambda g, b: (g, 0, 0)),
                  pl.BlockSpec((1, 1, LANES), lambda g, b: (g, 0, 0))],
        out_specs=pl.BlockSpec((nb, seq, LANES), lambda g, b: (b, 0, g)),
        out_shape=jax.ShapeDtypeStruct((bsz, seq, width), F32),
        scratch_shapes=2 * [pltpu.VMEM((nb * nrow, 2 * S5_SW), F32)],
        compiler_params=_cparams(("arbitrary", "arbitrary")),
        name="s5",
    )(u, w_a, w_o, lam, d_skip.reshape(S5_GB, 1, LANES))


def _gelu_tanh(x):
    return 0.5 * x * (1.0 + jnp.tanh(math.sqrt(2.0 / math.pi) * (x + 0.044715 * (x * x * x))))


def _merge_kernel(h_ref, oa_ref, ys_ref, gab_ref, mod_ref, wglu_ref, bglu_ref, wpa_ref, wpb_ref, wout_ref, o_ref):
    d = h_ref.shape[-1]
    yb = _gelu_tanh(ys_ref[0])
    yb = yb * jax.nn.sigmoid(_dot(yb.astype(BF16), wglu_ref[...]) + bglu_ref[...])
    y_b = _dot(yb.astype(BF16), wpb_ref[...])
    y_a = _dot(oa_ref[0], wpa_ref[...])
    gab = gab_ref[0].astype(F32)
    merged = jax.nn.sigmoid(gab[:, 0:d]) * y_a + jax.nn.sigmoid(gab[:, d:2 * d]) * y_b
    o_ref[0] = h_ref[0] + mod_ref[0, 5:6, :] * _dot(merged.astype(BF16), wout_ref[...])


def _merge_call(h, o_a, y_s, gab, mod, w_glu, b_glu, w_pa, w_pb, w_out):
    bsz, seq, d = h.shape
    tm = min(MERGE_TM, seq)
    tok = lambda w: pl.BlockSpec((1, tm, w), lambda b, i: (b, i, 0))
    return pl.pallas_call(
        _merge_kernel,
        grid=(bsz, seq // tm),
        in_specs=[tok(d), tok(o_a.shape[-1]), tok(y_s.shape[-1]), tok(gab.shape[-1]),
                  pl.BlockSpec((1, N_MOD, d), lambda b, i: (b, 0, 0)),
                  _resident(w_glu.shape), _resident((1, b_glu.shape[-1])), _resident(w_pa.shape),
                  _resident(w_pb.shape), _resident(w_out.shape)],
        out_specs=tok(d),
        out_shape=jax.ShapeDtypeStruct((bsz, seq, d), F32),
        compiler_params=_cparams(("parallel", "parallel")),
        name="merge",
    )(h, o_a, y_s, gab, mod, w_glu, b_glu.reshape(1, -1), w_pa, w_pb, w_out)


def _regroup_w_in(w_in):
    w = DN_WIDTH
    ba = w_in[:, 4 * w:4 * w + 2 * DN_HEADS]
    ba = jnp.pad(ba, ((0, 0), (0, LANES - 2 * DN_HEADS)))
    s5_off = 4 * w + 2 * DN_HEADS
    return jnp.concatenate([w_in[:, 0:3 * w], w_in[:, 3 * w:4 * w], ba,
                            w_in[:, s5_off:s5_off + S5_WIDTH], w_in[:, s5_off + S5_WIDTH:]], axis=1).astype(BF16)


def kernel(x, c, w_ada, b_ada, g_ffn1, w1_ffn1, w3_ffn1, w2_ffn1, g_mix, w_in, conv_qkv, a_log, dt_bias, g_onorm, lam_re, lam_im, log_step, b_re, b_im, c_re, c_im, d_skip, w_glu, b_glu, w_proj_a, w_proj_b, w_out, g_ffn2, w1_ffn2, w3_ffn2, w2_ffn2, g_final):
    bsz, seq, d = x.shape
    depth = w_ada.shape[0]
    h = x
    for l in range(depth):
        last = l == depth - 1
        mod = _mod_call(c, w_ada[l], b_ada[l]).reshape(bsz, N_MOD, d)
        h = _ffn_call(h, mod, g_ffn1[l], w1_ffn1[l].astype(BF16), w3_ffn1[l].astype(BF16),
                      w2_ffn1[l].astype(BF16), g_final, row=0, final=False)
        qkv, z, ba, s5_in, gab = _inproj_call(h, mod, g_mix[l], _regroup_w_in(w_in[l]), conv_qkv[l])
        hp = jnp.zeros((SUBLANES, LANES), F32)
        hp = hp.at[0, DN_HEADS:2 * DN_HEADS].set(a_log[l]).at[1, DN_HEADS:2 * DN_HEADS].set(dt_bias[l])
        gon = jnp.tile(g_onorm[l], DN_HEADS).reshape(1, DN_WIDTH)
        o_a = _dn_call(qkv, ba, z, hp, gon)
        w_a, w_o, lam = _s5_prep_call(lam_re[l], lam_im[l], log_step[l], b_re[l], b_im[l], c_re[l], c_im[l])
        y_s = _s5_call(s5_in, w_a, w_o, lam, d_skip[l])
        h = _merge_call(h, o_a, y_s, gab, mod, w_glu[l].astype(BF16), b_glu[l], w_proj_a[l].astype(BF16),
                        w_proj_b[l].astype(BF16), w_out[l].astype(BF16))
        h = _ffn_call(h, mod, g_ffn2[l], w1_ffn2[l].astype(BF16), w3_ffn2[l].astype(BF16),
                      w2_ffn2[l].astype(BF16), g_final, row=6, final=last)
    if depth == 0:
        raise ValueError("depth must be positive")
    return h
```

```python
import functools
import math

import jax
import jax.numpy as jnp
from jax import lax
from jax.experimental import pallas as pl
from jax.experimental.pallas import tpu as pltpu

F32 = jnp.float32
BF16 = jnp.bfloat16
EPS = 1e-6

LANES = 128
SUBLANES = 8
MXU_N = 256
VMEM_LIMIT_BYTES = 56 * 1024 * 1024

N_MOD = 9
DN_HEADS = 8
DN_HEAD_DIM = 64
DN_WIDTH = DN_HEADS * DN_HEAD_DIM
CONV_WIDTH = 4
CHUNK = 64
N_PAIRS = DN_WIDTH // LANES
S5_GROUP_CH = 16
S5_STATE = 64
S5_WIDTH = 512
S5_GB = S5_WIDTH // LANES
S5_GPB = LANES // S5_GROUP_CH
S5_L = 8
S5_SW = S5_GPB * S5_STATE
S5_NB = 4

FFN_TM = 1024
FFN_SPLIT = 2
INPROJ_TM = 512
INPROJ_CB = 256
DN_TS = 256
DN_NB = 4
MERGE_TM = 1024


def _dot(a, b):
    return jnp.dot(a, b, preferred_element_type=F32)


def _dot_nt(a, b):
    return lax.dot_general(a, b, (((1,), (1,)), ((), ())), preferred_element_type=F32)


def _dot_tn(a, b):
    return lax.dot_general(a, b, (((0,), (0,)), ((), ())), preferred_element_type=F32)


def _split3(x):
    hi = x.astype(BF16)
    r = x - hi.astype(F32)
    mid = r.astype(BF16)
    lo = (r - mid.astype(F32)).astype(BF16)
    return hi, mid, lo


def _dot_sel_r(x, sel):
    hi = x.astype(BF16)
    lo = (x - hi.astype(F32)).astype(BF16)
    return _dot(hi, sel) + _dot(lo, sel)


def _sigmoid(x):
    return 0.5 * jnp.tanh(0.5 * x) + 0.5


def _silu(x):
    return x * _sigmoid(x)


def _norm_mod(x, gain, shift, scale):
    ms = jnp.mean(x * x, axis=-1, keepdims=True)
    y = x * lax.rsqrt(ms + EPS) * gain
    return y * (1.0 + scale) + shift


def _cparams(sem):
    return pltpu.CompilerParams(dimension_semantics=sem, vmem_limit_bytes=VMEM_LIMIT_BYTES)


def _resident(shape):
    nd = len(shape)
    return pl.BlockSpec(shape, lambda *_: (0,) * nd, pipeline_mode=pl.Buffered(1))


def _mod_kernel(c_ref, w_ref, b_ref, o_ref):
    a = _silu(c_ref[...])
    o_ref[...] = jnp.dot(a, w_ref[...], preferred_element_type=F32,
                         precision=lax.Precision.HIGHEST) + b_ref[...]


def _mod_call(c, w_ada, b_ada):
    bsz, d = c.shape
    n = w_ada.shape[1]
    return pl.pallas_call(
        _mod_kernel,
        grid=(n // d,),
        in_specs=[pl.BlockSpec((bsz, d), lambda j: (0, 0)),
                  pl.BlockSpec((d, d), lambda j: (0, j)),
                  pl.BlockSpec((1, d), lambda j: (0, j))],
        out_specs=pl.BlockSpec((bsz, d), lambda j: (0, j)),
        out_shape=jax.ShapeDtypeStruct((bsz, n), F32),
        compiler_params=_cparams(("arbitrary",)),
        name="mod",
    )(c, w_ada, b_ada.reshape(1, n))


def _ffn_kernel(x_ref, mod_ref, g_ref, w1_ref, w3_ref, w2_ref, gf_ref, o_ref, *, row, final):
    x = x_ref[0]
    shift = mod_ref[0, row:row + 1, :]
    scale = mod_ref[0, row + 1:row + 2, :]
    gate = mod_ref[0, row + 2:row + 3, :]
    u = _norm_mod(x, g_ref[...], shift, scale).astype(BF16)
    f = w1_ref.shape[1]
    n_tiles = f // MXU_N
    bounds = [MXU_N * ((n_tiles * s + FFN_SPLIT - 1) // FFN_SPLIT) for s in range(FFN_SPLIT)] + [f]
    y = None
    for s in range(FFN_SPLIT):
        cols = slice(bounds[s], bounds[s + 1])
        a = (_silu(_dot(u, w1_ref[:, cols])) * _dot(u, w3_ref[:, cols])).astype(BF16)
        part = _dot(a, w2_ref[cols, :])
        y = part if y is None else y + part
    h = x + 0.5 * gate * y
    if final:
        ms = jnp.mean(h * h, axis=-1, keepdims=True)
        h = h * lax.rsqrt(ms + EPS) * gf_ref[...]
    o_ref[0] = h


def _ffn_call(x, mod, gain, w1, w3, w2, g_final, *, row, final):
    bsz, seq, d = x.shape
    f = w1.shape[1]
    tm = min(FFN_TM, seq)
    return pl.pallas_call(
        functools.partial(_ffn_kernel, row=row, final=final),
        grid=(bsz, seq // tm),
        in_specs=[pl.BlockSpec((1, tm, d), lambda b, i: (b, i, 0)),
                  pl.BlockSpec((1, N_MOD, d), lambda b, i: (b, 0, 0)),
                  _resident((1, d)), _resident((d, f)), _resident((d, f)), _resident((f, d)),
                  _resident((1, d))],
        out_specs=pl.BlockSpec((1, tm, d), lambda b, i: (b, i, 0)),
        out_shape=jax.ShapeDtypeStruct((bsz, seq, d), F32),
        compiler_params=_cparams(("parallel", "parallel")),
        name="ffn_final" if final else "ffn",
    )(x, mod, gain.reshape(1, d), w1, w3, w2, g_final.reshape(1, d))


_W_QKV = 3 * DN_WIDTH
_OFF_Z = _W_QKV
_OFF_BA = _OFF_Z + DN_WIDTH
_OFF_S5 = _OFF_BA + LANES
_OFF_G = _OFF_S5 + S5_WIDTH


def _inproj_kernel(h_ref, mod_ref, g_ref, w_ref, cw_ref, qkv_ref, z_ref, ba_ref, s5_ref, gab_ref, *xbufs, tm):
    @pl.when(pl.program_id(1) == 0)
    def _():
        for xbuf in xbufs:
            xbuf[0:SUBLANES, :] = jnp.zeros((SUBLANES, INPROJ_CB), F32)

    u = _norm_mod(h_ref[0], g_ref[...], mod_ref[0, 3:4, :], mod_ref[0, 4:5, :]).astype(BF16)

    def project(c0):
        xbuf = xbufs[c0 // INPROJ_CB]
        xbuf[SUBLANES:SUBLANES + tm, :] = _dot(u, w_ref[:, c0:c0 + INPROJ_CB])

    def conv_silu(c0, l0):
        xbuf = xbufs[c0 // INPROJ_CB]
        lanes = slice(l0, l0 + LANES)
        cols = slice(c0 + l0, c0 + l0 + LANES)
        acc = cw_ref[CONV_WIDTH - 1:CONV_WIDTH, cols] * xbuf[SUBLANES:SUBLANES + tm, lanes]
        for t in range(CONV_WIDTH - 1):
            off = SUBLANES - (CONV_WIDTH - 1) + t
            acc = acc + cw_ref[t:t + 1, cols] * xbuf[off:off + tm, lanes]
        xbuf[0:SUBLANES, lanes] = xbuf[tm:tm + SUBLANES, lanes]
        qkv_ref[0, :, cols] = acc * _sigmoid(acc)

    def plain(out_ref, off, c0, width):
        out_ref[0, :, c0:c0 + width] = _dot(u, w_ref[:, off + c0:off + c0 + width]).astype(out_ref.dtype)

    others = []
    for out_ref, off, total in ((z_ref, _OFF_Z, DN_WIDTH), (ba_ref, _OFF_BA, LANES),
                                (s5_ref, _OFF_S5, S5_WIDTH), (gab_ref, _OFF_G, gab_ref.shape[-1])):
        for c0 in range(0, total, INPROJ_CB):
            others.append(functools.partial(plain, out_ref, off, c0, min(INPROJ_CB, total - c0)))
    matmuls = []
    for c0 in range(0, _W_QKV, INPROJ_CB):
        matmuls.append(functools.partial(project, c0))
        matmuls.append(others.pop(0))
    matmuls += others
    convs = [functools.partial(conv_silu, c0, l0)
             for c0 in range(0, _W_QKV, INPROJ_CB) for l0 in range(0, INPROJ_CB, LANES)]
    lag = 2
    for n, task in enumerate(matmuls):
        task()
        if n >= lag and convs:
            convs.pop(0)()
    for piece in convs:
        piece()


def _inproj_call(h, mod, gain, w_cat, conv_w):
    bsz, seq, d = h.shape
    tm = min(INPROJ_TM, seq)
    wg = w_cat.shape[1] - _OFF_G
    widths = (_W_QKV, DN_WIDTH, LANES, S5_WIDTH, wg)
    dtypes = (F32, BF16, F32, F32, BF16)
    return pl.pallas_call(
        functools.partial(_inproj_kernel, tm=tm),
        grid=(bsz, seq // tm),
        in_specs=[pl.BlockSpec((1, tm, d), lambda b, i: (b, i, 0)),
                  pl.BlockSpec((1, N_MOD, d), lambda b, i: (b, 0, 0)),
                  _resident((1, d)), _resident(w_cat.shape), _resident((CONV_WIDTH, _W_QKV))],
        out_specs=[pl.BlockSpec((1, tm, w), lambda b, i: (b, i, 0)) for w in widths],
        out_shape=[jax.ShapeDtypeStruct((bsz, seq, w), t) for w, t in zip(widths, dtypes)],
        scratch_shapes=(_W_QKV // INPROJ_CB) * [pltpu.VMEM((tm + SUBLANES, INPROJ_CB), F32)],
        compiler_params=_cparams(("parallel", "arbitrary")),
        name="inproj",
    )(h, mod, gain.reshape(1, d), w_cat, conv_w)


def _seg_sum(x, seg):
    outs = [_dot(x[:, p * LANES:(p + 1) * LANES].astype(BF16), seg) for p in range(N_PAIRS)]
    return jnp.concatenate(outs, axis=1)


class _PerElement:
    def __init__(self, refs):
        self.refs = refs

    def __getitem__(self, idx):
        if isinstance(idx, tuple):
            return self.refs[idx[0]][idx[1:]]
        return self.refs[idx][...]

    def __setitem__(self, idx, val):
        if isinstance(idx, tuple):
            self.refs[idx[0]][idx[1:]] = val
        else:
            self.refs[idx][...] = val


_DN_N_SCRATCH = 14


def _dn_kernel(qkv_ref, ba_ref, z_ref, hp_ref, gon_ref, o_ref, *scratch, ts, nb):
    w = DN_WIDTH
    hd = DN_HEAD_DIM
    n_chunks = ts // CHUNK
    (st_ref, qn_s, kn_s, kb_s, kbg_s, vb_s, qd_s, kd_s, w_s, attn_s, gc_s, egl_s, o_s, u_s) = [
        _PerElement([scratch[j * _DN_N_SCRATCH + i] for j in range(nb)]) for i in range(_DN_N_SCRATCH)]

    @pl.when(pl.program_id(1) == 0)
    def _():
        for j in range(nb):
            st_ref[j] = jnp.zeros((N_PAIRS, LANES, LANES), F32)

    r128 = lax.broadcasted_iota(jnp.int32, (LANES, LANES), 0)
    c128 = lax.broadcasted_iota(jnp.int32, (LANES, LANES), 1)
    bd_mask = (r128 // hd) == (c128 // hd)
    seg = jnp.where(bd_mask, 1.0, 0.0).astype(BF16)
    er = lax.broadcasted_iota(jnp.int32, (LANES, w), 0)
    ec = lax.broadcasted_iota(jnp.int32, (LANES, w), 1) // hd
    exp_beta = jnp.where(er == ec, 1.0, 0.0).astype(BF16)
    exp_alpha = jnp.where(er - DN_HEADS == ec, 1.0, 0.0).astype(BF16)
    tr = lax.broadcasted_iota(jnp.int32, (ts, ts), 0)
    tc = lax.broadcasted_iota(jnp.int32, (ts, ts), 1)
    same_chunk = (tr // CHUNK) == (tc // CHUNK)
    tri_cum = jnp.where(same_chunk & (tr >= tc), 1.0, 0.0).astype(BF16)
    blk_ones = jnp.where(same_chunk, 1.0, 0.0).astype(BF16)

    ri = lax.broadcasted_iota(jnp.int32, (CHUNK, LANES), 0)
    li = lax.broadcasted_iota(jnp.int32, (CHUNK, LANES), 1)
    jj = li % hd
    causal = ri >= jj
    strict = ri > jj
    eye = ri == jj
    eye_f = jnp.where(eye, 1.0, 0.0)
    in16 = (ri // 16) == (jj // 16)
    in32 = (ri // 32) == (jj // 32)
    in32_only = in32 & jnp.logical_not(in16)
    head0 = li < hd
    zero_b = jnp.zeros((), BF16)

    def bd(y):
        yb = y.astype(BF16)
        return jnp.where(bd_mask, jnp.concatenate([yb, yb], axis=0), zero_b)

    def mm(x, y):
        return _dot(x.astype(BF16), bd(y))

    def tile(j, c, p):
        return (j, slice(c * CHUNK, (c + 1) * CHUNK), slice(p * LANES, (p + 1) * LANES))


    def prepare(j):
        ba = ba_ref[j]
        beta = _sigmoid(ba)
        xa = ba + hp_ref[1:2, :]
        softplus = jnp.maximum(xa, 0.0) + jnp.log1p(jnp.exp(-jnp.abs(xa)))
        log_alpha = -jnp.exp(hp_ref[0:1, :]) * softplus
        la_hi, la_mid, la_lo = _split3(log_alpha)
        g_cum = _dot(tri_cum, la_hi) + _dot(tri_cum, la_mid) + _dot(tri_cum, la_lo)
        g_last = _dot(blk_ones, la_hi) + _dot(blk_ones, la_mid) + _dot(blk_ones, la_lo)
        yield
        beta_f = _dot_sel_r(beta, exp_beta)
        yield
        gc_f = _dot_sel_r(g_cum, exp_alpha)
        yield
        gl_f = _dot_sel_r(g_last, exp_alpha)
        yield
        q = qkv_ref[j, :, 0:w]
        qn = q * (lax.rsqrt(_seg_sum(q * q, seg) + EPS) * (hd ** -0.5))
        qn_s[j] = qn.astype(BF16)
        yield
        k = qkv_ref[j, :, w:2 * w]
        kn = k * lax.rsqrt(_seg_sum(k * k, seg) + EPS)
        kn_s[j] = kn.astype(BF16)
        yield
        eg = jnp.exp(gc_f)
        qd_s[j] = (qn * eg).astype(BF16)
        gc_s[j] = gc_f
        yield
        kb = kn * beta_f
        kb_s[j] = kb.astype(BF16)
        kbg_s[j] = (kb * eg).astype(BF16)
        yield
        kd_s[j] = (kn * jnp.exp(gl_f - gc_f)).astype(BF16)
        egl_s[j] = jnp.exp(gl_f)
        yield
        vb_s[j] = (qkv_ref[j, :, 2 * w:3 * w] * beta_f).astype(BF16)
        yield

    def solve(j):
        items = [tile(j, c, p) for c in range(n_chunks) for p in range(N_PAIRS)]
        kn_c = [kn_s[it] for it in items]
        k2 = [jnp.concatenate([jnp.where(head0, x, zero_b), jnp.where(head0, zero_b, x)], axis=0) for x in kn_c]
        p2 = [_dot_nt(jnp.concatenate([kb_s[it], qn_s[it]], axis=0), y) for it, y in zip(items, k2)]
        yield
        a_mat = []
        for it, pp in zip(items, p2):
            gc_c = gc_s[it]
            gc_row = jnp.sum(jnp.where(eye, gc_c, 0.0), axis=0, keepdims=True)
            decay = jnp.exp(jnp.where(causal, gc_c - gc_row, -1e30))
            a_mat.append(jnp.where(strict, pp[0:CHUNK] * decay, 0.0))
            attn_s[it] = (pp[CHUNK:2 * CHUNK] * decay).astype(BF16)
        yield
        d = [jnp.where(in16, a, 0.0) for a in a_mat]
        d2 = [mm(x, x) for x in d]
        yield
        d4 = [mm(x, x) for x in d2]
        yield
        d8 = [mm(x, x) for x in d4]
        yield
        imd = [eye_f - x for x in d]
        p1 = [x + mm(x, y) for x, y in zip(imd, d2)]
        yield
        ipd4 = [eye_f + x for x in d4]
        pb = [x + mm(x, y) for x, y in zip(ipd4, d8)]
        yield
        t16 = [mm(x, y) for x, y in zip(p1, pb)]
        yield
        f32_ = [mm(t, jnp.where(in32_only, a, 0.0)) for t, a in zip(t16, a_mat)]
        yield
        t32 = [t - mm(f, t) for t, f in zip(t16, f32_)]
        yield
        f64_ = [mm(t, jnp.where(in32, 0.0, a)) for t, a in zip(t32, a_mat)]
        yield
        t64 = [t - mm(f, t) for t, f in zip(t32, f64_)]
        yield
        for it, t in zip(items, t64):
            rhs = jnp.concatenate([bd(vb_s[it]), bd(kbg_s[it])], axis=1)
            uw = _dot(t.astype(BF16), rhs)
            u_s[it] = uw[:, 0:LANES]
            w_s[it] = uw[:, LANES:2 * LANES].astype(BF16)
        yield

    def recur(j):
        state = [st_ref[j, p] for p in range(N_PAIRS)]
        for c in range(n_chunks):
            its = [tile(j, c, p) for p in range(N_PAIRS)]
            rs = [_dot(jnp.concatenate([w_s[it], qd_s[it]], axis=0), s.astype(BF16))
                  for it, s in zip(its, state)]
            yield
            v_new = [u_s[it] - r[0:CHUNK] for it, r in zip(its, rs)]
            upd = [_dot_tn(kd_s[it], v.astype(BF16)) for it, v in zip(its, v_new)]
            o_c = [r[CHUNK:2 * CHUNK] + _dot(attn_s[it], bd(v)) for it, r, v in zip(its, rs, v_new)]
            yield
            state = [s * egl_s[j, c * CHUNK:c * CHUNK + 1, it[2]] + jnp.where(bd_mask, x, 0.0)
                     for it, s, x in zip(its, state, upd)]
            for it, o in zip(its, o_c):
                o_s[it] = o
            yield
        for p in range(N_PAIRS):
            st_ref[j, p] = state[p]

    def finish(j):
        for c in range(n_chunks):
            o = o_s[j, c * CHUNK:(c + 1) * CHUNK, :]
            ms = _seg_sum(o * o, seg) * (1.0 / hd)
            zf = z_ref[j, c * CHUNK:(c + 1) * CHUNK, :].astype(F32)
            out = o * lax.rsqrt(ms + EPS) * gon_ref[...] * (zf * _sigmoid(zf))
            o_ref[j, c * CHUNK:(c + 1) * CHUNK, :] = out.astype(o_ref.dtype)
            yield

    for phase in (prepare, solve, recur, finish):
        _run_interleaved([phase(j) for j in range(nb)])


def _run_interleaved(gens):
    gens = list(gens)
    while gens:
        for g in list(gens):
            try:
                next(g)
            except StopIteration:
                gens.remove(g)


def _dn_call(qkv, ba, z, hp, gon):
    bsz, seq, _ = qkv.shape
    ts = min(DN_TS, seq)
    nb = DN_NB if bsz % DN_NB == 0 else 1
    w = DN_WIDTH
    bf = lambda: pltpu.VMEM((ts, w), BF16)
    ff = lambda: pltpu.VMEM((ts, w), F32)
    tok = lambda width: pl.BlockSpec((nb, ts, width), lambda b, i: (b, i, 0))
    return pl.pallas_call(
        functools.partial(_dn_kernel, ts=ts, nb=nb),
        grid=(bsz // nb, seq // ts),
        in_specs=[tok(3 * w), tok(LANES), tok(w), _resident((SUBLANES, LANES)), _resident((1, w))],
        out_specs=tok(w),
        out_shape=jax.ShapeDtypeStruct((bsz, seq, w), BF16),
        scratch_shapes=nb * [pltpu.VMEM((N_PAIRS, LANES, LANES), F32),
                             bf(), bf(), bf(), bf(), bf(), bf(), bf(), bf(), bf(), ff(), ff(), ff(), ff()],
        compiler_params=_cparams(("parallel", "arbitrary")),
        name="deltanet",
    )(qkv, ba, z, hp, gon)


def _s5_prep_kernel(lre_ref, lim_ref, lstep_ref, btr_ref, bti_ref, cxr_ref, cxi_ref, wa_ref, wot_ref, lam_ref):
    el = S5_L
    sw = S5_SW
    lr = jnp.minimum(lre_ref[0], -1e-4)
    li = lim_ref[0]
    step = jnp.exp(lstep_ref[0])
    zr = lr * step
    ang = li * step
    mag = jnp.exp(zr)
    lb_re = mag * jnp.cos(ang)
    lb_im = mag * jnp.sin(ang)
    den = lr * lr + li * li
    coef_re = ((lb_re - 1.0) * lr + lb_im * li) / den
    coef_im = (lb_im * lr - (lb_re - 1.0) * li) / den

    def power(tau):
        m = jnp.exp(tau * zr)
        return m * jnp.cos(tau * ang), m * jnp.sin(tau * ang)

    rg = lax.broadcasted_iota(jnp.int32, (LANES, sw), 0) // S5_GROUP_CH
    cg = lax.broadcasted_iota(jnp.int32, (LANES, sw), 1) // S5_STATE
    same = rg == cg

    def spread(ref):
        return jnp.where(same, jnp.concatenate([ref[0]] * S5_GPB, axis=0), 0.0)

    bt_re = spread(btr_ref)
    bt_im = spread(bti_ref)
    bb_re = coef_re * bt_re - coef_im * bt_im
    bb_im = coef_re * bt_im + coef_im * bt_re
    cx_re = spread(cxr_ref)
    cx_im = spread(cxi_ref)

    for t in range(el):
        pr, pi = power(float(el - 1 - t))
        rows = slice(t * LANES, (t + 1) * LANES)
        wa_ref[0, rows, el * LANES:el * LANES + sw] = (pr * bb_re - pi * bb_im).astype(wa_ref.dtype)
        wa_ref[0, rows, el * LANES + sw:el * LANES + 2 * sw] = (pr * bb_im + pi * bb_re).astype(wa_ref.dtype)

    cl = []
    for tau in range(el + 1):
        pr, pi = power(float(tau))
        cl.append((cx_re * pr - cx_im * pi, cx_re * pi + cx_im * pr))
    for t2 in range(el):
        q_re, q_im = cl[t2 + 1]
        rows = slice(t2 * LANES, (t2 + 1) * LANES)
        wot_ref[0, rows, 0:sw] = q_re.astype(wot_ref.dtype)
        wot_ref[0, rows, sw:2 * sw] = (-q_im).astype(wot_ref.dtype)

    def contract(a, b):
        return lax.dot_general(a, b, (((1,), (1,)), ((), ())), preferred_element_type=F32,
                               precision=lax.Precision.HIGHEST)

    kern = [(contract(bb_re, cl[tau][0]) - contract(bb_im, cl[tau][1])).astype(wa_ref.dtype) for tau in range(el)]
    zero = jnp.zeros((LANES, LANES), wa_ref.dtype)
    for t in range(el):
        for t2 in range(el):
            wa_ref[0, t * LANES:(t + 1) * LANES, t2 * LANES:(t2 + 1) * LANES] = kern[t2 - t] if t2 >= t else zero

    pr, pi = power(float(el))
    lam_ref[0, 0:1, :] = pr
    lam_ref[0, 1:2, :] = pi


def _s5_prep_call(lam_re, lam_im, log_step, b_re, b_im, c_re, c_im):
    g, p = lam_re.shape
    gc = b_re.shape[-1]
    el = S5_L
    row = lambda x: x.astype(F32).reshape(S5_GB, 1, S5_SW)
    bt = lambda x: x.astype(F32).reshape(S5_GB, S5_GPB, p, gc).transpose(0, 3, 1, 2).reshape(S5_GB, gc, S5_SW)
    cx = lambda x: x.astype(F32).reshape(S5_GB, S5_GPB, gc, p).transpose(0, 2, 1, 3).reshape(S5_GB, gc, S5_SW)
    vec = pl.BlockSpec((1, 1, S5_SW), lambda i: (i, 0, 0))
    mat = pl.BlockSpec((1, gc, S5_SW), lambda i: (i, 0, 0))
    wa_cols = el * LANES + 2 * S5_SW
    return pl.pallas_call(
        _s5_prep_kernel,
        grid=(S5_GB,),
        in_specs=[vec, vec, vec, mat, mat, mat, mat],
        out_specs=[pl.BlockSpec((1, el * LANES, wa_cols), lambda i: (i, 0, 0)),
                   pl.BlockSpec((1, el * LANES, 2 * S5_SW), lambda i: (i, 0, 0)),
                   pl.BlockSpec((1, 2, S5_SW), lambda i: (i, 0, 0))],
        out_shape=[jax.ShapeDtypeStruct((S5_GB, el * LANES, wa_cols), BF16),
                   jax.ShapeDtypeStruct((S5_GB, el * LANES, 2 * S5_SW), BF16),
                   jax.ShapeDtypeStruct((S5_GB, 2, S5_SW), F32)],
        compiler_params=_cparams(("arbitrary",)),
        name="s5_prep",
    )(row(lam_re), row(lam_im), row(jnp.repeat(log_step, p)), bt(b_re), bt(b_im), cx(c_re), cx(c_im))


def _s5_kernel(u_ref, wa_ref, wo_ref, lam_ref, d_ref, y_ref, s_s, xp_s, *, nrow, nb):
    el = S5_L
    sw = S5_SW
    xs = [u_ref[:, pl.ds(t, nrow, stride=el), :].reshape(nb * nrow, LANES) for t in range(el)]
    xcat = jnp.concatenate([x.astype(BF16) for x in xs], axis=1)
    r = _dot(xcat, wa_ref[0])
    s_s[...] = r[:, el * LANES:]
    lam_r = lam_ref[0, 0:1, :]
    lam_i = lam_ref[0, 1:2, :]

    def step(n, carry):
        new = []
        for j in range(nb):
            x_r, x_i = carry[2 * j], carry[2 * j + 1]
            row = pl.ds(j * nrow + n, 1)
            xp_s[row, 0:sw] = x_r
            xp_s[row, sw:2 * sw] = x_i
            new.append(lam_r * x_r - lam_i * x_i + s_s[row, 0:sw])
            new.append(lam_r * x_i + lam_i * x_r + s_s[row, sw:2 * sw])
        return tuple(new)

    zero = jnp.zeros((1, sw), F32)
    lax.fori_loop(0, nrow, step, (zero,) * (2 * nb), unroll=4)
    y = r[:, 0:el * LANES] + _dot_nt(xp_s[...].astype(BF16), wo_ref[0])
    dsk = d_ref[0]
    for t in range(el):
        y_t = y[:, t * LANES:(t + 1) * LANES] + dsk * xs[t]
        y_ref[:, pl.ds(t, nrow, stride=el), :] = y_t.reshape(nb, nrow, LANES)


def _s5_call(u, w_a, w_o, lam, d_skip):
    bsz, seq, width = u.shape
    nrow = seq // S5_L
    nb = S5_NB if bsz % S5_NB == 0 else 1
    return pl.pallas_call(
        functools.partial(_s5_kernel, nrow=nrow, nb=nb),
        grid=(S5_GB, bsz // nb),
        in_specs=[pl.BlockSpec((nb, seq, LANES), lambda g, b: (b, 0, g)),
                  pl.BlockSpec((1,) + w_a.shape[1:], lambda g, b: (g, 0, 0)),
                  pl.BlockSpec((1,) + w_o.shape[1:], lambda g, b: (g, 0, 0)),
                  pl.BlockSpec((1, 2, S5_SW), lambda g, b: (g, 0, 0)),
                  pl.BlockSpec((1, 1, LANES), lambda g, b: (g, 0, 0))],
        out_specs=pl.BlockSpec((nb, seq, LANES), lambda g, b: (b, 0, g)),
        out_shape=jax.ShapeDtypeStruct((bsz, seq, width), F32),
        scratch_shapes=2 * [pltpu.VMEM((nb * nrow, 2 * S5_SW), F32)],
        compiler_params=_cparams(("arbitrary", "arbitrary")),
        name="s5",
    )(u, w_a, w_o, lam, d_skip.reshape(S5_GB, 1, LANES))


def _gelu_tanh(x):
    return 0.5 * x * (1.0 + jnp.tanh(math.sqrt(2.0 / math.pi) * (x + 0.044715 * (x * x * x))))


def _merge_kernel(h_ref, oa_ref, ys_ref, gab_ref, mod_ref, wglu_ref, bglu_ref, wpa_ref, wpb_ref, wout_ref, o_ref):
    d = h_ref.shape[-1]
    yb = _gelu_tanh(ys_ref[0])
    yb = yb * _sigmoid(_dot(yb.astype(BF16), wglu_ref[...]) + bglu_ref[...])
    y_b = _dot(yb.astype(BF16), wpb_ref[...])
    y_a = _dot(oa_ref[0], wpa_ref[...])
    gab = gab_ref[0].astype(F32)
    merged = _sigmoid(gab[:, 0:d]) * y_a + _sigmoid(gab[:, d:2 * d]) * y_b
    o_ref[0] = h_ref[0] + mod_ref[0, 5:6, :] * _dot(merged.astype(BF16), wout_ref[...])


def _merge_call(h, o_a, y_s, gab, mod, w_glu, b_glu, w_pa, w_pb, w_out):
    bsz, seq, d = h.shape
    tm = min(MERGE_TM, seq)
    tok = lambda w: pl.BlockSpec((1, tm, w), lambda b, i: (b, i, 0))
    return pl.pallas_call(
        _merge_kernel,
        grid=(bsz, seq // tm),
        in_specs=[tok(d), tok(o_a.shape[-1]), tok(y_s.shape[-1]), tok(gab.shape[-1]),
                  pl.BlockSpec((1, N_MOD, d), lambda b, i: (b, 0, 0)),
                  _resident(w_glu.shape), _resident((1, b_glu.shape[-1])), _resident(w_pa.shape),
                  _resident(w_pb.shape), _resident(w_out.shape)],
        out_specs=tok(d),
        out_shape=jax.ShapeDtypeStruct((bsz, seq, d), F32),
        compiler_params=_cparams(("parallel", "parallel")),
        name="merge",
    )(h, o_a, y_s, gab, mod, w_glu, b_glu.reshape(1, -1), w_pa, w_pb, w_out)


def _regroup_w_in(w_in):
    w = DN_WIDTH
    ba = w_in[:, 4 * w:4 * w + 2 * DN_HEADS]
    ba = jnp.pad(ba, ((0, 0), (0, LANES - 2 * DN_HEADS)))
    s5_off = 4 * w + 2 * DN_HEADS
    return jnp.concatenate([w_in[:, 0:3 * w], w_in[:, 3 * w:4 * w], ba,
                            w_in[:, s5_off:s5_off + S5_WIDTH], w_in[:, s5_off + S5_WIDTH:]], axis=1).astype(BF16)


def kernel(x, c, w_ada, b_ada, g_ffn1, w1_ffn1, w3_ffn1, w2_ffn1, g_mix, w_in, conv_qkv, a_log, dt_bias, g_onorm, lam_re, lam_im, log_step, b_re, b_im, c_re, c_im, d_skip, w_glu, b_glu, w_proj_a, w_proj_b, w_out, g_ffn2, w1_ffn2, w3_ffn2, w2_ffn2, g_final):
    bsz, seq, d = x.shape
    depth = w_ada.shape[0]
    h = x
    for l in range(depth):
        last = l == depth - 1
        mod = _mod_call(c, w_ada[l], b_ada[l]).reshape(bsz, N_MOD, d)
        h = _ffn_call(h, mod, g_ffn1[l], w1_ffn1[l].astype(BF16), w3_ffn1[l].astype(BF16),
                      w2_ffn1[l].astype(BF16), g_final, row=0, final=False)
        qkv, z, ba, s5_in, gab = _inproj_call(h, mod, g_mix[l], _regroup_w_in(w_in[l]), conv_qkv[l])
        hp = jnp.zeros((SUBLANES, LANES), F32)
        hp = hp.at[0, DN_HEADS:2 * DN_HEADS].set(a_log[l]).at[1, DN_HEADS:2 * DN_HEADS].set(dt_bias[l])
        gon = jnp.tile(g_onorm[l], DN_HEADS).reshape(1, DN_WIDTH)
        o_a = _dn_call(qkv, ba, z, hp, gon)
        w_a, w_o, lam = _s5_prep_call(lam_re[l], lam_im[l], log_step[l], b_re[l], b_im[l], c_re[l], c_im[l])
        y_s = _s5_call(s5_in, w_a, w_o, lam, d_skip[l])
        h = _merge_call(h, o_a, y_s, gab, mod, w_glu[l].astype(BF16), b_glu[l], w_proj_a[l].astype(BF16),
                        w_proj_b[l].astype(BF16), w_out[l].astype(BF16))
        h = _ffn_call(h, mod, g_ffn2[l], w1_ffn2[l].astype(BF16), w3_ffn2[l].astype(BF16),
                      w2_ffn2[l].astype(BF16), g_final, row=6, final=last)
    if depth == 0:
        raise ValueError("depth must be positive")
    return h
```

```python
import functools
import math

import jax
import jax.numpy as jnp
from jax import lax
from jax.experimental import pallas as pl
from jax.experimental.pallas import tpu as pltpu

F32 = jnp.float32
BF16 = jnp.bfloat16
EPS = 1e-6

LANES = 128
SUBLANES = 8
MXU_N = 256
VMEM_LIMIT_BYTES = 56 * 1024 * 1024

N_MOD = 9
DN_HEADS = 8
DN_HEAD_DIM = 64
DN_WIDTH = DN_HEADS * DN_HEAD_DIM
CONV_WIDTH = 4
CHUNK = 64
N_PAIRS = DN_WIDTH // LANES
S5_GROUP_CH = 16
S5_STATE = 64
S5_WIDTH = 512
S5_GB = S5_WIDTH // LANES
S5_GPB = LANES // S5_GROUP_CH
S5_L = 8
S5_SW = S5_GPB * S5_STATE
S5_NB = 4

FFN_TM = 1024
FFN_SPLIT = 2
INPROJ_TM = 512
INPROJ_CB = 256
DN_TS = 256
DN_NB = 4
MERGE_TM = 1024


def _dot(a, b):
    return jnp.dot(a, b, preferred_element_type=F32)


def _dot_nt(a, b):
    return lax.dot_general(a, b, (((1,), (1,)), ((), ())), preferred_element_type=F32)


def _dot_tn(a, b):
    return lax.dot_general(a, b, (((0,), (0,)), ((), ())), preferred_element_type=F32)


def _split3(x):
    hi = x.astype(BF16)
    r = x - hi.astype(F32)
    mid = r.astype(BF16)
    lo = (r - mid.astype(F32)).astype(BF16)
    return hi, mid, lo


def _dot_sel_r(x, sel):
    hi = x.astype(BF16)
    lo = (x - hi.astype(F32)).astype(BF16)
    return _dot(hi, sel) + _dot(lo, sel)


def _sigmoid(x):
    return 0.5 * jnp.tanh(0.5 * x) + 0.5


def _silu(x):
    return x * _sigmoid(x)


def _norm_mod(x, gain, shift, scale):
    ms = jnp.mean(x * x, axis=-1, keepdims=True)
    y = x * lax.rsqrt(ms + EPS) * gain
    return y * (1.0 + scale) + shift


def _cparams(sem):
    return pltpu.CompilerParams(dimension_semantics=sem, vmem_limit_bytes=VMEM_LIMIT_BYTES)


def _resident(shape):
    nd = len(shape)
    return pl.BlockSpec(shape, lambda *_: (0,) * nd, pipeline_mode=pl.Buffered(1))


def _mod_kernel(c_ref, w_ref, b_ref, o_ref):
    a = _silu(c_ref[...])
    o_ref[...] = jnp.dot(a, w_ref[...], preferred_element_type=F32,
                         precision=lax.Precision.HIGHEST) + b_ref[...]


def _mod_call(c, w_ada, b_ada):
    bsz, d = c.shape
    n = w_ada.shape[1]
    return pl.pallas_call(
        _mod_kernel,
        grid=(n // d,),
        in_specs=[pl.BlockSpec((bsz, d), lambda j: (0, 0)),
                  pl.BlockSpec((d, d), lambda j: (0, j)),
                  pl.BlockSpec((1, d), lambda j: (0, j))],
        out_specs=pl.BlockSpec((bsz, d), lambda j: (0, j)),
        out_shape=jax.ShapeDtypeStruct((bsz, n), F32),
        compiler_params=_cparams(("arbitrary",)),
        name="mod",
    )(c, w_ada, b_ada.reshape(1, n))


def _ffn_kernel(x_ref, mod_ref, g_ref, w1_ref, w3_ref, w2_ref, gf_ref, o_ref, *, row, final):
    x = x_ref[0]
    shift = mod_ref[0, row:row + 1, :]
    scale = mod_ref[0, row + 1:row + 2, :]
    gate = mod_ref[0, row + 2:row + 3, :]
    u = _norm_mod(x, g_ref[...], shift, scale).astype(BF16)
    f = w1_ref.shape[1]
    n_tiles = f // MXU_N
    bounds = [MXU_N * ((n_tiles * s + FFN_SPLIT - 1) // FFN_SPLIT) for s in range(FFN_SPLIT)] + [f]
    y = None
    for s in range(FFN_SPLIT):
        cols = slice(bounds[s], bounds[s + 1])
        a = (_silu(_dot(u, w1_ref[:, cols])) * _dot(u, w3_ref[:, cols])).astype(BF16)
        part = _dot(a, w2_ref[cols, :])
        y = part if y is None else y + part
    h = x + 0.5 * gate * y
    if final:
        ms = jnp.mean(h * h, axis=-1, keepdims=True)
        h = h * lax.rsqrt(ms + EPS) * gf_ref[...]
    o_ref[0] = h


def _ffn_call(x, mod, gain, w1, w3, w2, g_final, *, row, final):
    bsz, seq, d = x.shape
    f = w1.shape[1]
    tm = min(FFN_TM, seq)
    return pl.pallas_call(
        functools.partial(_ffn_kernel, row=row, final=final),
        grid=(bsz, seq // tm),
        in_specs=[pl.BlockSpec((1, tm, d), lambda b, i: (b, i, 0)),
                  pl.BlockSpec((1, N_MOD, d), lambda b, i: (b, 0, 0)),
                  _resident((1, d)), _resident((d, f)), _resident((d, f)), _resident((f, d)),
                  _resident((1, d))],
        out_specs=pl.BlockSpec((1, tm, d), lambda b, i: (b, i, 0)),
        out_shape=jax.ShapeDtypeStruct((bsz, seq, d), F32),
        compiler_params=_cparams(("parallel", "parallel")),
        name="ffn_final" if final else "ffn",
    )(x, mod, gain.reshape(1, d), w1, w3, w2, g_final.reshape(1, d))


_W_QKV = 3 * DN_WIDTH
_OFF_Z = _W_QKV
_OFF_BA = _OFF_Z + DN_WIDTH
_OFF_S5 = _OFF_BA + LANES
_OFF_G = _OFF_S5 + S5_WIDTH


def _inproj_kernel(h_ref, mod_ref, g_ref, w_ref, cw_ref, qkv_ref, z_ref, ba_ref, s5_ref, gab_ref, *xbufs, tm):
    @pl.when(pl.program_id(1) == 0)
    def _():
        for xbuf in xbufs:
            xbuf[0:SUBLANES, :] = jnp.zeros((SUBLANES, INPROJ_CB), F32)

    u = _norm_mod(h_ref[0], g_ref[...], mod_ref[0, 3:4, :], mod_ref[0, 4:5, :]).astype(BF16)

    def project(c0):
        xbuf = xbufs[c0 // INPROJ_CB]
        xbuf[SUBLANES:SUBLANES + tm, :] = _dot(u, w_ref[:, c0:c0 + INPROJ_CB])

    def conv_silu(c0, l0):
        xbuf = xbufs[c0 // INPROJ_CB]
        lanes = slice(l0, l0 + LANES)
        cols = slice(c0 + l0, c0 + l0 + LANES)
        acc = cw_ref[CONV_WIDTH - 1:CONV_WIDTH, cols] * xbuf[SUBLANES:SUBLANES + tm, lanes]
        for t in range(CONV_WIDTH - 1):
            off = SUBLANES - (CONV_WIDTH - 1) + t
            acc = acc + cw_ref[t:t + 1, cols] * xbuf[off:off + tm, lanes]
        xbuf[0:SUBLANES, lanes] = xbuf[tm:tm + SUBLANES, lanes]
        qkv_ref[0, :, cols] = acc * _sigmoid(acc)

    def plain(out_ref, off, c0, width):
        out_ref[0, :, c0:c0 + width] = _dot(u, w_ref[:, off + c0:off + c0 + width]).astype(out_ref.dtype)

    others = []
    for out_ref, off, total in ((z_ref, _OFF_Z, DN_WIDTH), (ba_ref, _OFF_BA, LANES),
                                (s5_ref, _OFF_S5, S5_WIDTH), (gab_ref, _OFF_G, gab_ref.shape[-1])):
        for c0 in range(0, total, INPROJ_CB):
            others.append(functools.partial(plain, out_ref, off, c0, min(INPROJ_CB, total - c0)))
    matmuls = []
    for c0 in range(0, _W_QKV, INPROJ_CB):
        matmuls.append(functools.partial(project, c0))
        matmuls.append(others.pop(0))
    matmuls += others
    convs = [functools.partial(conv_silu, c0, l0)
             for c0 in range(0, _W_QKV, INPROJ_CB) for l0 in range(0, INPROJ_CB, LANES)]
    lag = 2
    for n, task in enumerate(matmuls):
        task()
        if n >= lag and convs:
            convs.pop(0)()
    for piece in convs:
        piece()


def _inproj_call(h, mod, gain, w_cat, conv_w):
    bsz, seq, d = h.shape
    tm = min(INPROJ_TM, seq)
    wg = w_cat.shape[1] - _OFF_G
    widths = (_W_QKV, DN_WIDTH, LANES, S5_WIDTH, wg)
    dtypes = (F32, BF16, F32, F32, BF16)
    return pl.pallas_call(
        functools.partial(_inproj_kernel, tm=tm),
        grid=(bsz, seq // tm),
        in_specs=[pl.BlockSpec((1, tm, d), lambda b, i: (b, i, 0)),
                  pl.BlockSpec((1, N_MOD, d), lambda b, i: (b, 0, 0)),
                  _resident((1, d)), _resident(w_cat.shape), _resident((CONV_WIDTH, _W_QKV))],
        out_specs=[pl.BlockSpec((1, tm, w), lambda b, i: (b, i, 0)) for w in widths],
        out_shape=[jax.ShapeDtypeStruct((bsz, seq, w), t) for w, t in zip(widths, dtypes)],
        scratch_shapes=(_W_QKV // INPROJ_CB) * [pltpu.VMEM((tm + SUBLANES, INPROJ_CB), F32)],
        compiler_params=_cparams(("parallel", "arbitrary")),
        name="inproj",
    )(h, mod, gain.reshape(1, d), w_cat, conv_w)


def _seg_sum(x, seg):
    outs = [_dot(x[:, p * LANES:(p + 1) * LANES].astype(BF16), seg) for p in range(N_PAIRS)]
    return jnp.concatenate(outs, axis=1)


class _PerElement:
    def __init__(self, refs):
        self.refs = refs

    def __getitem__(self, idx):
        if isinstance(idx, tuple):
            return self.refs[idx[0]][idx[1:]]
        return self.refs[idx][...]

    def __setitem__(self, idx, val):
        if isinstance(idx, tuple):
            self.refs[idx[0]][idx[1:]] = val
        else:
            self.refs[idx][...] = val


_DN_N_SCRATCH = 14


def _dn_kernel(qkv_ref, ba_ref, z_ref, hp_ref, gon_ref, o_ref, *scratch, ts, nb):
    w = DN_WIDTH
    hd = DN_HEAD_DIM
    n_chunks = ts // CHUNK
    (st_ref, qn_s, kn_s, kb_s, kbg_s, vb_s, qd_s, kd_s, w_s, attn_s, gc_s, egl_s, o_s, u_s) = [
        _PerElement([scratch[j * _DN_N_SCRATCH + i] for j in range(nb)]) for i in range(_DN_N_SCRATCH)]

    @pl.when(pl.program_id(1) == 0)
    def _():
        for j in range(nb):
            st_ref[j] = jnp.zeros((N_PAIRS, LANES, LANES), F32)

    r128 = lax.broadcasted_iota(jnp.int32, (LANES, LANES), 0)
    c128 = lax.broadcasted_iota(jnp.int32, (LANES, LANES), 1)
    bd_mask = (r128 // hd) == (c128 // hd)
    seg = jnp.where(bd_mask, 1.0, 0.0).astype(BF16)
    er = lax.broadcasted_iota(jnp.int32, (LANES, w), 0)
    ec = lax.broadcasted_iota(jnp.int32, (LANES, w), 1) // hd
    exp_beta = jnp.where(er == ec, 1.0, 0.0).astype(BF16)
    exp_alpha = jnp.where(er - DN_HEADS == ec, 1.0, 0.0).astype(BF16)
    tr = lax.broadcasted_iota(jnp.int32, (ts, ts), 0)
    tc = lax.broadcasted_iota(jnp.int32, (ts, ts), 1)
    same_chunk = (tr // CHUNK) == (tc // CHUNK)
    tri_cum = jnp.where(same_chunk & (tr >= tc), 1.0, 0.0).astype(BF16)
    blk_ones = jnp.where(same_chunk, 1.0, 0.0).astype(BF16)

    ri = lax.broadcasted_iota(jnp.int32, (CHUNK, LANES), 0)
    li = lax.broadcasted_iota(jnp.int32, (CHUNK, LANES), 1)
    jj = li % hd
    causal = ri >= jj
    strict = ri > jj
    eye = ri == jj
    eye_f = jnp.where(eye, 1.0, 0.0)
    in16 = (ri // 16) == (jj // 16)
    in32 = (ri // 32) == (jj // 32)
    in32_only = in32 & jnp.logical_not(in16)
    head0 = li < hd
    zero_b = jnp.zeros((), BF16)

    def bd(y):
        yb = y.astype(BF16)
        return jnp.where(bd_mask, jnp.concatenate([yb, yb], axis=0), zero_b)

    def mm(x, y):
        return _dot(x.astype(BF16), bd(y))

    def tile(j, c, p):
        return (j, slice(c * CHUNK, (c + 1) * CHUNK), slice(p * LANES, (p + 1) * LANES))


    def prepare(j):
        ba = ba_ref[j]
        beta = _sigmoid(ba)
        xa = ba + hp_ref[1:2, :]
        softplus = jnp.maximum(xa, 0.0) + jnp.log1p(jnp.exp(-jnp.abs(xa)))
        log_alpha = -jnp.exp(hp_ref[0:1, :]) * softplus
        la_hi, la_mid, la_lo = _split3(log_alpha)
        g_cum = _dot(tri_cum, la_hi) + _dot(tri_cum, la_mid) + _dot(tri_cum, la_lo)
        g_last = _dot(blk_ones, la_hi) + _dot(blk_ones, la_mid) + _dot(blk_ones, la_lo)
        yield
        beta_f = _dot_sel_r(beta, exp_beta)
        yield
        gc_f = _dot_sel_r(g_cum, exp_alpha)
        yield
        gl_f = _dot_sel_r(g_last, exp_alpha)
        yield
        q = qkv_ref[j, :, 0:w]
        qn = q * (lax.rsqrt(_seg_sum(q * q, seg) + EPS) * (hd ** -0.5))
        qn_s[j] = qn.astype(BF16)
        yield
        k = qkv_ref[j, :, w:2 * w]
        kn = k * lax.rsqrt(_seg_sum(k * k, seg) + EPS)
        kn_s[j] = kn.astype(BF16)
        yield
        eg = jnp.exp(gc_f)
        qd_s[j] = (qn * eg).astype(BF16)
        gc_s[j] = gc_f
        yield
        kb = kn * beta_f
        kb_s[j] = kb.astype(BF16)
        kbg_s[j] = (kb * eg).astype(BF16)
        yield
        kd_s[j] = (kn * jnp.exp(gl_f - gc_f)).astype(BF16)
        egl_s[j] = jnp.exp(gl_f)
        yield
        vb_s[j] = (qkv_ref[j, :, 2 * w:3 * w] * beta_f).astype(BF16)
        yield

    def solve(j):
        items = [tile(j, c, p) for c in range(n_chunks) for p in range(N_PAIRS)]
        kn_c = [kn_s[it] for it in items]
        k2 = [jnp.concatenate([jnp.where(head0, x, zero_b), jnp.where(head0, zero_b, x)], axis=0) for x in kn_c]
        p2 = [_dot_nt(jnp.concatenate([kb_s[it], qn_s[it]], axis=0), y) for it, y in zip(items, k2)]
        yield
        a_mat = []
        for it, pp in zip(items, p2):
            gc_c = gc_s[it]
            gc_row = jnp.sum(jnp.where(eye, gc_c, 0.0), axis=0, keepdims=True)
            decay = jnp.exp(jnp.where(causal, gc_c - gc_row, -1e30))
            a_mat.append(jnp.where(strict, pp[0:CHUNK] * decay, 0.0))
            attn_s[it] = (pp[CHUNK:2 * CHUNK] * decay).astype(BF16)
        yield
        d = [jnp.where(in16, a, 0.0) for a in a_mat]
        d2 = [mm(x, x) for x in d]
        yield
        imd = [eye_f - x for x in d]
        both = [mm(jnp.concatenate([x, y], axis=0), x) for x, y in zip(d2, imd)]
        d4 = [b[0:CHUNK] for b in both]
        p1 = [y + b[CHUNK:2 * CHUNK] for y, b in zip(imd, both)]
        yield
        both = [mm(jnp.concatenate([x, y], axis=0), x) for x, y in zip(d4, p1)]
        d8 = [b[0:CHUNK] for b in both]
        q1 = [y + b[CHUNK:2 * CHUNK] for y, b in zip(p1, both)]
        yield
        t16 = [x + mm(x, y) for x, y in zip(q1, d8)]
        yield
        f32_ = [mm(t, jnp.where(in32_only, a, 0.0)) for t, a in zip(t16, a_mat)]
        yield
        t32 = [t - mm(f, t) for t, f in zip(t16, f32_)]
        yield
        f64_ = [mm(t, jnp.where(in32, 0.0, a)) for t, a in zip(t32, a_mat)]
        yield
        t64 = [t - mm(f, t) for t, f in zip(t32, f64_)]
        yield
        for it, t in zip(items, t64):
            rhs = jnp.concatenate([bd(vb_s[it]), bd(kbg_s[it])], axis=1)
            uw = _dot(t.astype(BF16), rhs)
            u_s[it] = uw[:, 0:LANES]
            w_s[it] = uw[:, LANES:2 * LANES].astype(BF16)
        yield

    def recur(j):
        state = [st_ref[j, p] for p in range(N_PAIRS)]
        for c in range(n_chunks):
            its = [tile(j, c, p) for p in range(N_PAIRS)]
            rs = [_dot(jnp.concatenate([w_s[it], qd_s[it]], axis=0), s.astype(BF16))
                  for it, s in zip(its, state)]
            yield
            v_new = [u_s[it] - r[0:CHUNK] for it, r in zip(its, rs)]
            upd = [_dot_tn(kd_s[it], v.astype(BF16)) for it, v in zip(its, v_new)]
            o_c = [r[CHUNK:2 * CHUNK] + _dot(attn_s[it], bd(v)) for it, r, v in zip(its, rs, v_new)]
            yield
            state = [s * egl_s[j, c * CHUNK:c * CHUNK + 1, it[2]] + jnp.where(bd_mask, x, 0.0)
                     for it, s, x in zip(its, state, upd)]
            for it, o in zip(its, o_c):
                o_s[it] = o
            yield
        for p in range(N_PAIRS):
            st_ref[j, p] = state[p]

    def finish(j):
        for c in range(n_chunks):
            o = o_s[j, c * CHUNK:(c + 1) * CHUNK, :]
            ms = _seg_sum(o * o, seg) * (1.0 / hd)
            zf = z_ref[j, c * CHUNK:(c + 1) * CHUNK, :].astype(F32)
            out = o * lax.rsqrt(ms + EPS) * gon_ref[...] * (zf * _sigmoid(zf))
            o_ref[j, c * CHUNK:(c + 1) * CHUNK, :] = out.astype(o_ref.dtype)
            yield

    for phase in (prepare, solve, recur, finish):
        _run_interleaved([phase(j) for j in range(nb)])


def _run_interleaved(gens):
    gens = list(gens)
    while gens:
        for g in list(gens):
            try:
                next(g)
            except StopIteration:
                gens.remove(g)


def _dn_call(qkv, ba, z, hp, gon):
    bsz, seq, _ = qkv.shape
    ts = min(DN_TS, seq)
    nb = DN_NB if bsz % DN_NB == 0 else 1
    w = DN_WIDTH
    bf = lambda: pltpu.VMEM((ts, w), BF16)
    ff = lambda: pltpu.VMEM((ts, w), F32)
    tok = lambda width: pl.BlockSpec((nb, ts, width), lambda b, i: (b, i, 0))
    return pl.pallas_call(
        functools.partial(_dn_kernel, ts=ts, nb=nb),
        grid=(bsz // nb, seq // ts),
        in_specs=[tok(3 * w), tok(LANES), tok(w), _resident((SUBLANES, LANES)), _resident((1, w))],
        out_specs=tok(w),
        out_shape=jax.ShapeDtypeStruct((bsz, seq, w), BF16),
        scratch_shapes=nb * [pltpu.VMEM((N_PAIRS, LANES, LANES), F32),
                             bf(), bf(), bf(), bf(), bf(), bf(), bf(), bf(), bf(), ff(), ff(), ff(), ff()],
        compiler_params=_cparams(("parallel", "arbitrary")),
        name="deltanet",
    )(qkv, ba, z, hp, gon)


def _s5_prep_kernel(lre_ref, lim_ref, lstep_ref, btr_ref, bti_ref, cxr_ref, cxi_ref, wa_ref, wot_ref, lam_ref):
    el = S5_L
    sw = S5_SW
    lr = jnp.minimum(lre_ref[0], -1e-4)
    li = lim_ref[0]
    step = jnp.exp(lstep_ref[0])
    zr = lr * step
    ang = li * step
    mag = jnp.exp(zr)
    lb_re = mag * jnp.cos(ang)
    lb_im = mag * jnp.sin(ang)
    den = lr * lr + li * li
    coef_re = ((lb_re - 1.0) * lr + lb_im * li) / den
    coef_im = (lb_im * lr - (lb_re - 1.0) * li) / den

    def power(tau):
        m = jnp.exp(tau * zr)
        return m * jnp.cos(tau * ang), m * jnp.sin(tau * ang)

    rg = lax.broadcasted_iota(jnp.int32, (LANES, sw), 0) // S5_GROUP_CH
    cg = lax.broadcasted_iota(jnp.int32, (LANES, sw), 1) // S5_STATE
    same = rg == cg

    def spread(ref):
        return jnp.where(same, jnp.concatenate([ref[0]] * S5_GPB, axis=0), 0.0)

    bt_re = spread(btr_ref)
    bt_im = spread(bti_ref)
    bb_re = coef_re * bt_re - coef_im * bt_im
    bb_im = coef_re * bt_im + coef_im * bt_re
    cx_re = spread(cxr_ref)
    cx_im = spread(cxi_ref)

    for t in range(el):
        pr, pi = power(float(el - 1 - t))
        rows = slice(t * LANES, (t + 1) * LANES)
        wa_ref[0, rows, el * LANES:el * LANES + sw] = (pr * bb_re - pi * bb_im).astype(wa_ref.dtype)
        wa_ref[0, rows, el * LANES + sw:el * LANES + 2 * sw] = (pr * bb_im + pi * bb_re).astype(wa_ref.dtype)

    cl = []
    for tau in range(el + 1):
        pr, pi = power(float(tau))
        cl.append((cx_re * pr - cx_im * pi, cx_re * pi + cx_im * pr))
    for t2 in range(el):
        q_re, q_im = cl[t2 + 1]
        rows = slice(t2 * LANES, (t2 + 1) * LANES)
        wot_ref[0, rows, 0:sw] = q_re.astype(wot_ref.dtype)
        wot_ref[0, rows, sw:2 * sw] = (-q_im).astype(wot_ref.dtype)

    def contract(a, b):
        return lax.dot_general(a, b, (((1,), (1,)), ((), ())), preferred_element_type=F32,
                               precision=lax.Precision.HIGHEST)

    kern = [(contract(bb_re, cl[tau][0]) - contract(bb_im, cl[tau][1])).astype(wa_ref.dtype) for tau in range(el)]
    zero = jnp.zeros((LANES, LANES), wa_ref.dtype)
    for t in range(el):
        for t2 in range(el):
            wa_ref[0, t * LANES:(t + 1) * LANES, t2 * LANES:(t2 + 1) * LANES] = kern[t2 - t] if t2 >= t else zero

    pr, pi = power(float(el))
    lam_ref[0, 0:1, :] = pr
    lam_ref[0, 1:2, :] = pi


def _s5_prep_call(lam_re, lam_im, log_step, b_re, b_im, c_re, c_im):
    g, p = lam_re.shape
    gc = b_re.shape[-1]
    el = S5_L
    row = lambda x: x.astype(F32).reshape(S5_GB, 1, S5_SW)
    bt = lambda x: x.astype(F32).reshape(S5_GB, S5_GPB, p, gc).transpose(0, 3, 1, 2).reshape(S5_GB, gc, S5_SW)
    cx = lambda x: x.astype(F32).reshape(S5_GB, S5_GPB, gc, p).transpose(0, 2, 1, 3).reshape(S5_GB, gc, S5_SW)
    vec = pl.BlockSpec((1, 1, S5_SW), lambda i: (i, 0, 0))
    mat = pl.BlockSpec((1, gc, S5_SW), lambda i: (i, 0, 0))
    wa_cols = el * LANES + 2 * S5_SW
    return pl.pallas_call(
        _s5_prep_kernel,
        grid=(S5_GB,),
        in_specs=[vec, vec, vec, mat, mat, mat, mat],
        out_specs=[pl.BlockSpec((1, el * LANES, wa_cols), lambda i: (i, 0, 0)),
                   pl.BlockSpec((1, el * LANES, 2 * S5_SW), lambda i: (i, 0, 0)),
                   pl.BlockSpec((1, 2, S5_SW), lambda i: (i, 0, 0))],
        out_shape=[jax.ShapeDtypeStruct((S5_GB, el * LANES, wa_cols), BF16),
                   jax.ShapeDtypeStruct((S5_GB, el * LANES, 2 * S5_SW), BF16),
                   jax.ShapeDtypeStruct((S5_GB, 2, S5_SW), F32)],
        compiler_params=_cparams(("arbitrary",)),
        name="s5_prep",
    )(row(lam_re), row(lam_im), row(jnp.repeat(log_step, p)), bt(b_re), bt(b_im), cx(c_re), cx(c_im))


def _s5_kernel(u_ref, wa_ref, wo_ref, lam_ref, d_ref, y_ref, s_s, xp_s, *, nrow, nb):
    el = S5_L
    sw = S5_SW
    xs = [u_ref[:, pl.ds(t, nrow, stride=el), :].reshape(nb * nrow, LANES) for t in range(el)]
    xcat = jnp.concatenate([x.astype(BF16) for x in xs], axis=1)
    r = _dot(xcat, wa_ref[0])
    s_s[...] = r[:, el * LANES:]
    lam_r = lam_ref[0, 0:1, :]
    lam_i = lam_ref[0, 1:2, :]

    def step(n, carry):
        new = []
        for j in range(nb):
            x_r, x_i = carry[2 * j], carry[2 * j + 1]
            row = pl.ds(j * nrow + n, 1)
            xp_s[row, 0:sw] = x_r
            xp_s[row, sw:2 * sw] = x_i
            new.append(lam_r * x_r - lam_i * x_i + s_s[row, 0:sw])
            new.append(lam_r * x_i + lam_i * x_r + s_s[row, sw:2 * sw])
        return tuple(new)

    zero = jnp.zeros((1, sw), F32)
    lax.fori_loop(0, nrow, step, (zero,) * (2 * nb), unroll=4)
    y = r[:, 0:el * LANES] + _dot_nt(xp_s[...].astype(BF16), wo_ref[0])
    dsk = d_ref[0]
    for t in range(el):
        y_t = y[:, t * LANES:(t + 1) * LANES] + dsk * xs[t]
        y_ref[:, pl.ds(t, nrow, stride=el), :] = y_t.reshape(nb, nrow, LANES)


def _s5_call(u, w_a, w_o, lam, d_skip):
    bsz, seq, width = u.shape
    nrow = seq // S5_L
    nb = S5_NB if bsz % S5_NB == 0 else 1
    return pl.pallas_call(
        functools.partial(_s5_kernel, nrow=nrow, nb=nb),
        grid=(S5_GB, bsz // nb),
        in_specs=[pl.BlockSpec((nb, seq, LANES), lambda g, b: (b, 0, g)),
                  pl.BlockSpec((1,) + w_a.shape[1:], lambda g, b: (g, 0, 0)),
                  pl.BlockSpec((1,) + w_o.shape[1:], lambda g, b: (g, 0, 0)),
                  pl.BlockSpec((1, 2, S5_SW), lambda g, b: (g, 0, 0)),
                  pl.BlockSpec((1, 1, LANES), lambda g, b: (g, 0, 0))],
        out_specs=pl.BlockSpec((nb, seq, LANES), lambda g, b: (b, 0, g)),
        out_shape=jax.ShapeDtypeStruct((bsz, seq, width), F32),
        scratch_shapes=2 * [pltpu.VMEM((nb * nrow, 2 * S5_SW), F32)],
        compiler_params=_cparams(("arbitrary", "arbitrary")),
        name="s5",
    )(u, w_a, w_o, lam, d_skip.reshape(S5_GB, 1, LANES))


def _gelu_tanh(x):
    return 0.5 * x * (1.0 + jnp.tanh(math.sqrt(2.0 / math.pi) * (x + 0.044715 * (x * x * x))))


def _merge_kernel(h_ref, oa_ref, ys_ref, gab_ref, mod_ref, wglu_ref, bglu_ref, wpa_ref, wpb_ref, wout_ref, o_ref):
    d = h_ref.shape[-1]
    yb = _gelu_tanh(ys_ref[0])
    yb = yb * _sigmoid(_dot(yb.astype(BF16), wglu_ref[...]) + bglu_ref[...])
    y_b = _dot(yb.astype(BF16), wpb_ref[...])
    y_a = _dot(oa_ref[0], wpa_ref[...])
    gab = gab_ref[0].astype(F32)
    merged = _sigmoid(gab[:, 0:d]) * y_a + _sigmoid(gab[:, d:2 * d]) * y_b
    o_ref[0] = h_ref[0] + mod_ref[0, 5:6, :] * _dot(merged.astype(BF16), wout_ref[...])


def _merge_call(h, o_a, y_s, gab, mod, w_glu, b_glu, w_pa, w_pb, w_out):
    bsz, seq, d = h.shape
    tm = min(MERGE_TM, seq)
    tok = lambda w: pl.BlockSpec((1, tm, w), lambda b, i: (b, i, 0))
    return pl.pallas_call(
        _merge_kernel,
        grid=(bsz, seq // tm),
        in_specs=[tok(d), tok(o_a.shape[-1]), tok(y_s.shape[-1]), tok(gab.shape[-1]),
                  pl.BlockSpec((1, N_MOD, d), lambda b, i: (b, 0, 0)),
                  _resident(w_glu.shape), _resident((1, b_glu.shape[-1])), _resident(w_pa.shape),
                  _resident(w_pb.shape), _resident(w_out.shape)],
        out_specs=tok(d),
        out_shape=jax.ShapeDtypeStruct((bsz, seq, d), F32),
        compiler_params=_cparams(("parallel", "parallel")),
        name="merge",
    )(h, o_a, y_s, gab, mod, w_glu, b_glu.reshape(1, -1), w_pa, w_pb, w_out)


def _regroup_w_in(w_in):
    w = DN_WIDTH
    ba = w_in[:, 4 * w:4 * w + 2 * DN_HEADS]
    ba = jnp.pad(ba, ((0, 0), (0, LANES - 2 * DN_HEADS)))
    s5_off = 4 * w + 2 * DN_HEADS
    return jnp.concatenate([w_in[:, 0:3 * w], w_in[:, 3 * w:4 * w], ba,
                            w_in[:, s5_off:s5_off + S5_WIDTH], w_in[:, s5_off + S5_WIDTH:]], axis=1).astype(BF16)


def kernel(x, c, w_ada, b_ada, g_ffn1, w1_ffn1, w3_ffn1, w2_ffn1, g_mix, w_in, conv_qkv, a_log, dt_bias, g_onorm, lam_re, lam_im, log_step, b_re, b_im, c_re, c_im, d_skip, w_glu, b_glu, w_proj_a, w_proj_b, w_out, g_ffn2, w1_ffn2, w3_ffn2, w2_ffn2, g_final):
    bsz, seq, d = x.shape
    depth = w_ada.shape[0]
    h = x
    for l in range(depth):
        last = l == depth - 1
        mod = _mod_call(c, w_ada[l], b_ada[l]).reshape(bsz, N_MOD, d)
        h = _ffn_call(h, mod, g_ffn1[l], w1_ffn1[l].astype(BF16), w3_ffn1[l].astype(BF16),
                      w2_ffn1[l].astype(BF16), g_final, row=0, final=False)
        qkv, z, ba, s5_in, gab = _inproj_call(h, mod, g_mix[l], _regroup_w_in(w_in[l]), conv_qkv[l])
        hp = jnp.zeros((SUBLANES, LANES), F32)
        hp = hp.at[0, DN_HEADS:2 * DN_HEADS].set(a_log[l]).at[1, DN_HEADS:2 * DN_HEADS].set(dt_bias[l])
        gon = jnp.tile(g_onorm[l], DN_HEADS).reshape(1, DN_WIDTH)
        o_a = _dn_call(qkv, ba, z, hp, gon)
        w_a, w_o, lam = _s5_prep_call(lam_re[l], lam_im[l], log_step[l], b_re[l], b_im[l], c_re[l], c_im[l])
        y_s = _s5_call(s5_in, w_a, w_o, lam, d_skip[l])
        h = _merge_call(h, o_a, y_s, gab, mod, w_glu[l].astype(BF16), b_glu[l], w_proj_a[l].astype(BF16),
                        w_proj_b[l].astype(BF16), w_out[l].astype(BF16))
        h = _ffn_call(h, mod, g_ffn2[l], w1_ffn2[l].astype(BF16), w3_ffn2[l].astype(BF16),
                      w2_ffn2[l].astype(BF16), g_final, row=6, final=last)
    if depth == 0:
        raise ValueError("depth must be positive")
    return h
```

```python
import functools
import math

import jax
import jax.numpy as jnp
from jax import lax
from jax.experimental import pallas as pl
from jax.experimental.pallas import tpu as pltpu

F32 = jnp.float32
BF16 = jnp.bfloat16
EPS = 1e-6

LANES = 128
SUBLANES = 8
MXU_N = 256
VMEM_LIMIT_BYTES = 56 * 1024 * 1024

N_MOD = 9
DN_HEADS = 8
DN_HEAD_DIM = 64
DN_WIDTH = DN_HEADS * DN_HEAD_DIM
CONV_WIDTH = 4
CHUNK = 64
N_PAIRS = DN_WIDTH // LANES
S5_GROUP_CH = 16
S5_STATE = 64
S5_WIDTH = 512
S5_GB = S5_WIDTH // LANES
S5_GPB = LANES // S5_GROUP_CH
S5_L = 8
S5_SW = S5_GPB * S5_STATE
S5_NB = 4

FFN_TM = 1024
FFN_SPLIT = 2
INPROJ_TM = 512
INPROJ_CB = 256
DN_TS = 256
DN_NB = 4
MERGE_TM = 1024


def _dot(a, b):
    return jnp.dot(a, b, preferred_element_type=F32)


def _dot_nt(a, b):
    return lax.dot_general(a, b, (((1,), (1,)), ((), ())), preferred_element_type=F32)


def _dot_tn(a, b):
    return lax.dot_general(a, b, (((0,), (0,)), ((), ())), preferred_element_type=F32)


def _split3(x):
    hi = x.astype(BF16)
    r = x - hi.astype(F32)
    mid = r.astype(BF16)
    lo = (r - mid.astype(F32)).astype(BF16)
    return hi, mid, lo


def _dot_sel_r(x, sel):
    hi = x.astype(BF16)
    lo = (x - hi.astype(F32)).astype(BF16)
    return _dot(hi, sel) + _dot(lo, sel)


def _sigmoid(x):
    return 0.5 * jnp.tanh(0.5 * x) + 0.5


def _silu(x):
    return x * _sigmoid(x)


def _norm_mod(x, gain, shift, scale):
    ms = jnp.mean(x * x, axis=-1, keepdims=True)
    y = x * lax.rsqrt(ms + EPS) * gain
    return y * (1.0 + scale) + shift


def _cparams(sem):
    return pltpu.CompilerParams(dimension_semantics=sem, vmem_limit_bytes=VMEM_LIMIT_BYTES)


def _resident(shape):
    nd = len(shape)
    return pl.BlockSpec(shape, lambda *_: (0,) * nd, pipeline_mode=pl.Buffered(1))


def _mod_kernel(c_ref, w_ref, b_ref, o_ref):
    a = _silu(c_ref[...])
    o_ref[...] = jnp.dot(a, w_ref[...], preferred_element_type=F32,
                         precision=lax.Precision.HIGHEST) + b_ref[...]


def _mod_call(c, w_ada, b_ada):
    bsz, d = c.shape
    n = w_ada.shape[1]
    return pl.pallas_call(
        _mod_kernel,
        grid=(n // d,),
        in_specs=[pl.BlockSpec((bsz, d), lambda j: (0, 0)),
                  pl.BlockSpec((d, d), lambda j: (0, j)),
                  pl.BlockSpec((1, d), lambda j: (0, j))],
        out_specs=pl.BlockSpec((bsz, d), lambda j: (0, j)),
        out_shape=jax.ShapeDtypeStruct((bsz, n), F32),
        compiler_params=_cparams(("arbitrary",)),
        name="mod",
    )(c, w_ada, b_ada.reshape(1, n))


def _ffn_kernel(x_ref, mod_ref, g_ref, w1_ref, w3_ref, w2_ref, gf_ref, o_ref, *, row, final):
    x = x_ref[0]
    shift = mod_ref[0, row:row + 1, :]
    scale = mod_ref[0, row + 1:row + 2, :]
    gate = mod_ref[0, row + 2:row + 3, :]
    u = _norm_mod(x, g_ref[...], shift, scale).astype(BF16)
    f = w1_ref.shape[1]
    n_tiles = f // MXU_N
    bounds = [MXU_N * ((n_tiles * s + FFN_SPLIT - 1) // FFN_SPLIT) for s in range(FFN_SPLIT)] + [f]
    y = None
    for s in range(FFN_SPLIT):
        cols = slice(bounds[s], bounds[s + 1])
        a = (_silu(_dot(u, w1_ref[:, cols])) * _dot(u, w3_ref[:, cols])).astype(BF16)
        part = _dot(a, w2_ref[cols, :])
        y = part if y is None else y + part
    h = x + 0.5 * gate * y
    if final:
        ms = jnp.mean(h * h, axis=-1, keepdims=True)
        h = h * lax.rsqrt(ms + EPS) * gf_ref[...]
    o_ref[0] = h


def _ffn_call(x, mod, gain, w1, w3, w2, g_final, *, row, final):
    bsz, seq, d = x.shape
    f = w1.shape[1]
    tm = min(FFN_TM, seq)
    return pl.pallas_call(
        functools.partial(_ffn_kernel, row=row, final=final),
        grid=(bsz, seq // tm),
        in_specs=[pl.BlockSpec((1, tm, d), lambda b, i: (b, i, 0)),
                  pl.BlockSpec((1, N_MOD, d), lambda b, i: (b, 0, 0)),
                  _resident((1, d)), _resident((d, f)), _resident((d, f)), _resident((f, d)),
                  _resident((1, d))],
        out_specs=pl.BlockSpec((1, tm, d), lambda b, i: (b, i, 0)),
        out_shape=jax.ShapeDtypeStruct((bsz, seq, d), F32),
        compiler_params=_cparams(("parallel", "parallel")),
        name="ffn_final" if final else "ffn",
    )(x, mod, gain.reshape(1, d), w1, w3, w2, g_final.reshape(1, d))


_W_QKV = 3 * DN_WIDTH
_OFF_Z = _W_QKV
_OFF_BA = _OFF_Z + DN_WIDTH
_OFF_S5 = _OFF_BA + LANES
_OFF_G = _OFF_S5 + S5_WIDTH


def _inproj_kernel(h_ref, mod_ref, g_ref, w_ref, cw_ref, qkv_ref, z_ref, ba_ref, s5_ref, gab_ref, *xbufs, tm):
    @pl.when(pl.program_id(1) == 0)
    def _():
        for xbuf in xbufs:
            xbuf[0:SUBLANES, :] = jnp.zeros((SUBLANES, INPROJ_CB), F32)

    u = _norm_mod(h_ref[0], g_ref[...], mod_ref[0, 3:4, :], mod_ref[0, 4:5, :]).astype(BF16)

    def project(c0):
        xbuf = xbufs[c0 // INPROJ_CB]
        xbuf[SUBLANES:SUBLANES + tm, :] = _dot(u, w_ref[:, c0:c0 + INPROJ_CB])

    def conv_silu(c0, l0):
        xbuf = xbufs[c0 // INPROJ_CB]
        lanes = slice(l0, l0 + LANES)
        cols = slice(c0 + l0, c0 + l0 + LANES)
        acc = cw_ref[CONV_WIDTH - 1:CONV_WIDTH, cols] * xbuf[SUBLANES:SUBLANES + tm, lanes]
        for t in range(CONV_WIDTH - 1):
            off = SUBLANES - (CONV_WIDTH - 1) + t
            acc = acc + cw_ref[t:t + 1, cols] * xbuf[off:off + tm, lanes]
        xbuf[0:SUBLANES, lanes] = xbuf[tm:tm + SUBLANES, lanes]
        qkv_ref[0, :, cols] = acc * _sigmoid(acc)

    def plain(out_ref, off, c0, width):
        out_ref[0, :, c0:c0 + width] = _dot(u, w_ref[:, off + c0:off + c0 + width]).astype(out_ref.dtype)

    others = []
    for out_ref, off, total in ((z_ref, _OFF_Z, DN_WIDTH), (ba_ref, _OFF_BA, LANES),
                                (s5_ref, _OFF_S5, S5_WIDTH), (gab_ref, _OFF_G, gab_ref.shape[-1])):
        for c0 in range(0, total, INPROJ_CB):
            others.append(functools.partial(plain, out_ref, off, c0, min(INPROJ_CB, total - c0)))
    matmuls = []
    for c0 in range(0, _W_QKV, INPROJ_CB):
        matmuls.append(functools.partial(project, c0))
        matmuls.append(others.pop(0))
    matmuls += others
    convs = [functools.partial(conv_silu, c0, l0)
             for c0 in range(0, _W_QKV, INPROJ_CB) for l0 in range(0, INPROJ_CB, LANES)]
    lag = 2
    for n, task in enumerate(matmuls):
        task()
        if n >= lag and convs:
            convs.pop(0)()
    for piece in convs:
        piece()


def _inproj_call(h, mod, gain, w_cat, conv_w):
    bsz, seq, d = h.shape
    tm = min(INPROJ_TM, seq)
    wg = w_cat.shape[1] - _OFF_G
    widths = (_W_QKV, DN_WIDTH, LANES, S5_WIDTH, wg)
    dtypes = (F32, BF16, F32, F32, BF16)
    return pl.pallas_call(
        functools.partial(_inproj_kernel, tm=tm),
        grid=(bsz, seq // tm),
        in_specs=[pl.BlockSpec((1, tm, d), lambda b, i: (b, i, 0)),
                  pl.BlockSpec((1, N_MOD, d), lambda b, i: (b, 0, 0)),
                  _resident((1, d)), _resident(w_cat.shape), _resident((CONV_WIDTH, _W_QKV))],
        out_specs=[pl.BlockSpec((1, tm, w), lambda b, i: (b, i, 0)) for w in widths],
        out_shape=[jax.ShapeDtypeStruct((bsz, seq, w), t) for w, t in zip(widths, dtypes)],
        scratch_shapes=(_W_QKV // INPROJ_CB) * [pltpu.VMEM((tm + SUBLANES, INPROJ_CB), F32)],
        compiler_params=_cparams(("parallel", "arbitrary")),
        name="inproj",
    )(h, mod, gain.reshape(1, d), w_cat, conv_w)


def _seg_sum(x, seg):
    outs = [_dot(x[:, p * LANES:(p + 1) * LANES].astype(BF16), seg) for p in range(N_PAIRS)]
    return jnp.concatenate(outs, axis=1)


class _PerElement:
    def __init__(self, refs):
        self.refs = refs

    def __getitem__(self, idx):
        if isinstance(idx, tuple):
            return self.refs[idx[0]][idx[1:]]
        return self.refs[idx][...]

    def __setitem__(self, idx, val):
        if isinstance(idx, tuple):
            self.refs[idx[0]][idx[1:]] = val
        else:
            self.refs[idx][...] = val


_DN_N_SCRATCH = 14


def _dn_kernel(qkv_ref, ba_ref, z_ref, hp_ref, gon_ref, o_ref, *scratch, ts, nb):
    w = DN_WIDTH
    hd = DN_HEAD_DIM
    n_chunks = ts // CHUNK
    (st_ref, qn_s, kn_s, kb_s, kbg_s, vb_s, qd_s, kd_s, w_s, attn_s, gc_s, egl_s, o_s, u_s) = [
        _PerElement([scratch[j * _DN_N_SCRATCH + i] for j in range(nb)]) for i in range(_DN_N_SCRATCH)]

    @pl.when(pl.program_id(1) == 0)
    def _():
        for j in range(nb):
            st_ref[j] = jnp.zeros((N_PAIRS, LANES, LANES), F32)

    r128 = lax.broadcasted_iota(jnp.int32, (LANES, LANES), 0)
    c128 = lax.broadcasted_iota(jnp.int32, (LANES, LANES), 1)
    bd_mask = (r128 // hd) == (c128 // hd)
    seg = jnp.where(bd_mask, 1.0, 0.0).astype(BF16)
    er = lax.broadcasted_iota(jnp.int32, (LANES, w), 0)
    ec = lax.broadcasted_iota(jnp.int32, (LANES, w), 1) // hd
    exp_beta = jnp.where(er == ec, 1.0, 0.0).astype(BF16)
    exp_alpha = jnp.where(er - DN_HEADS == ec, 1.0, 0.0).astype(BF16)
    tr = lax.broadcasted_iota(jnp.int32, (ts, ts), 0)
    tc = lax.broadcasted_iota(jnp.int32, (ts, ts), 1)
    same_chunk = (tr // CHUNK) == (tc // CHUNK)
    tri_cum = jnp.where(same_chunk & (tr >= tc), 1.0, 0.0).astype(BF16)

    ri = lax.broadcasted_iota(jnp.int32, (CHUNK, LANES), 0)
    li = lax.broadcasted_iota(jnp.int32, (CHUNK, LANES), 1)
    jj = li % hd
    causal = ri >= jj
    strict = ri > jj
    eye = ri == jj
    eye_f = jnp.where(eye, 1.0, 0.0)
    in16 = (ri // 16) == (jj // 16)
    in32 = (ri // 32) == (jj // 32)
    in32_only = in32 & jnp.logical_not(in16)
    head0 = li < hd
    zero_b = jnp.zeros((), BF16)

    def bd(y):
        yb = y.astype(BF16)
        return jnp.where(bd_mask, jnp.concatenate([yb, yb], axis=0), zero_b)

    def mm(x, y):
        return _dot(x.astype(BF16), bd(y))

    def tile(j, c, p):
        return (j, slice(c * CHUNK, (c + 1) * CHUNK), slice(p * LANES, (p + 1) * LANES))


    def prepare(j):
        ba = ba_ref[j]
        beta = _sigmoid(ba)
        xa = ba + hp_ref[1:2, :]
        softplus = jnp.maximum(xa, 0.0) + jnp.log1p(jnp.exp(-jnp.abs(xa)))
        log_alpha = -jnp.exp(hp_ref[0:1, :]) * softplus
        la_hi, la_mid, la_lo = _split3(log_alpha)
        g_cum = _dot(tri_cum, la_hi) + _dot(tri_cum, la_mid) + _dot(tri_cum, la_lo)
        yield
        beta_f = _dot_sel_r(beta, exp_beta)
        yield
        gc_f = _dot_sel_r(g_cum, exp_alpha)
        yield
        q = qkv_ref[j, :, 0:w]
        qn = q * (lax.rsqrt(_seg_sum(q * q, seg) + EPS) * (hd ** -0.5))
        qn_s[j] = qn.astype(BF16)
        yield
        k = qkv_ref[j, :, w:2 * w]
        kn = k * lax.rsqrt(_seg_sum(k * k, seg) + EPS)
        kn_s[j] = kn.astype(BF16)
        yield
        eg = jnp.exp(gc_f)
        qd_s[j] = (qn * eg).astype(BF16)
        gc_s[j] = gc_f
        yield
        kb = kn * beta_f
        kb_s[j] = kb.astype(BF16)
        kbg_s[j] = (kb * eg).astype(BF16)
        yield
        for c in range(n_chunks):
            rows = slice(c * CHUNK, (c + 1) * CHUNK)
            g_tot = gc_s[j, (c + 1) * CHUNK - 1:(c + 1) * CHUNK, :]
            kd_s[j, rows, :] = (kn[rows] * jnp.exp(g_tot - gc_f[rows])).astype(BF16)
            egl_s[j, c * SUBLANES:c * SUBLANES + 1, :] = jnp.exp(g_tot)
        yield
        vb_s[j] = (qkv_ref[j, :, 2 * w:3 * w] * beta_f).astype(BF16)
        yield

    def solve(j):
        items = [tile(j, c, p) for c in range(n_chunks) for p in range(N_PAIRS)]
        kn_c = [kn_s[it] for it in items]
        k2 = [jnp.concatenate([jnp.where(head0, x, zero_b), jnp.where(head0, zero_b, x)], axis=0) for x in kn_c]
        p2 = [_dot_nt(jnp.concatenate([kb_s[it], qn_s[it]], axis=0), y) for it, y in zip(items, k2)]
        yield
        a_mat = []
        for it, pp in zip(items, p2):
            gc_c = gc_s[it]
            gc_row = jnp.sum(jnp.where(eye, gc_c, 0.0), axis=0, keepdims=True)
            decay = jnp.exp(jnp.where(causal, gc_c - gc_row, -1e30))
            a_mat.append(jnp.where(strict, pp[0:CHUNK] * decay, 0.0))
            attn_s[it] = (pp[CHUNK:2 * CHUNK] * decay).astype(BF16)
        yield
        d = [jnp.where(in16, a, 0.0) for a in a_mat]
        d2 = [mm(x, x) for x in d]
        yield
        imd = [eye_f - x for x in d]
        both = [mm(jnp.concatenate([x, y], axis=0), x) for x, y in zip(d2, imd)]
        d4 = [b[0:CHUNK] for b in both]
        p1 = [y + b[CHUNK:2 * CHUNK] for y, b in zip(imd, both)]
        yield
        both = [mm(jnp.concatenate([x, y], axis=0), x) for x, y in zip(d4, p1)]
        d8 = [b[0:CHUNK] for b in both]
        q1 = [y + b[CHUNK:2 * CHUNK] for y, b in zip(p1, both)]
        yield
        t16 = [x + mm(x, y) for x, y in zip(q1, d8)]
        yield
        f32_ = [mm(t, jnp.where(in32_only, a, 0.0)) for t, a in zip(t16, a_mat)]
        yield
        t32 = [t - mm(f, t) for t, f in zip(t16, f32_)]
        yield
        f64_ = [mm(t, jnp.where(in32, 0.0, a)) for t, a in zip(t32, a_mat)]
        yield
        t64 = [t - mm(f, t) for t, f in zip(t32, f64_)]
        yield
        for it, t in zip(items, t64):
            rhs = jnp.concatenate([bd(vb_s[it]), bd(kbg_s[it])], axis=1)
            uw = _dot(t.astype(BF16), rhs)
            u_s[it] = uw[:, 0:LANES]
            w_s[it] = uw[:, LANES:2 * LANES].astype(BF16)
        yield

    def recur(j):
        state = [st_ref[j, p] for p in range(N_PAIRS)]
        for c in range(n_chunks):
            its = [tile(j, c, p) for p in range(N_PAIRS)]
            rs = [_dot(jnp.concatenate([w_s[it], qd_s[it]], axis=0), s.astype(BF16))
                  for it, s in zip(its, state)]
            yield
            v_new = [u_s[it] - r[0:CHUNK] for it, r in zip(its, rs)]
            upd = [_dot_tn(kd_s[it], v.astype(BF16)) for it, v in zip(its, v_new)]
            o_c = [r[CHUNK:2 * CHUNK] + _dot(attn_s[it], bd(v)) for it, r, v in zip(its, rs, v_new)]
            yield
            state = [s * egl_s[j, c * SUBLANES:c * SUBLANES + 1, it[2]] + jnp.where(bd_mask, x, 0.0)
                     for it, s, x in zip(its, state, upd)]
            for it, o in zip(its, o_c):
                o_s[it] = o
            yield
        for p in range(N_PAIRS):
            st_ref[j, p] = state[p]

    def finish(j):
        for c in range(n_chunks):
            o = o_s[j, c * CHUNK:(c + 1) * CHUNK, :]
            ms = _seg_sum(o * o, seg) * (1.0 / hd)
            zf = z_ref[j, c * CHUNK:(c + 1) * CHUNK, :].astype(F32)
            out = o * lax.rsqrt(ms + EPS) * gon_ref[...] * (zf * _sigmoid(zf))
            o_ref[j, c * CHUNK:(c + 1) * CHUNK, :] = out.astype(o_ref.dtype)
            yield

    for phase in (prepare, solve, recur, finish):
        _run_interleaved([phase(j) for j in range(nb)])


def _run_interleaved(gens):
    gens = list(gens)
    while gens:
        for g in list(gens):
            try:
                next(g)
            except StopIteration:
                gens.remove(g)


def _dn_call(qkv, ba, z, hp, gon):
    bsz, seq, _ = qkv.shape
    ts = min(DN_TS, seq)
    nb = DN_NB if bsz % DN_NB == 0 else 1
    w = DN_WIDTH
    bf = lambda: pltpu.VMEM((ts, w), BF16)
    ff = lambda: pltpu.VMEM((ts, w), F32)
    tok = lambda width: pl.BlockSpec((nb, ts, width), lambda b, i: (b, i, 0))
    return pl.pallas_call(
        functools.partial(_dn_kernel, ts=ts, nb=nb),
        grid=(bsz // nb, seq // ts),
        in_specs=[tok(3 * w), tok(LANES), tok(w), _resident((SUBLANES, LANES)), _resident((1, w))],
        out_specs=tok(w),
        out_shape=jax.ShapeDtypeStruct((bsz, seq, w), BF16),
        scratch_shapes=nb * [pltpu.VMEM((N_PAIRS, LANES, LANES), F32),
                             bf(), bf(), bf(), bf(), bf(), bf(), bf(), bf(), bf(), ff(),
                             pltpu.VMEM((ts // CHUNK * SUBLANES, w), F32), ff(), ff()],
        compiler_params=_cparams(("parallel", "arbitrary")),
        name="deltanet",
    )(qkv, ba, z, hp, gon)


def _s5_prep_kernel(lre_ref, lim_ref, lstep_ref, btr_ref, bti_ref, cxr_ref, cxi_ref, wa_ref, wot_ref, lam_ref):
    el = S5_L
    sw = S5_SW
    lr = jnp.minimum(lre_ref[0], -1e-4)
    li = lim_ref[0]
    step = jnp.exp(lstep_ref[0])
    zr = lr * step
    ang = li * step
    mag = jnp.exp(zr)
    lb_re = mag * jnp.cos(ang)
    lb_im = mag * jnp.sin(ang)
    den = lr * lr + li * li
    coef_re = ((lb_re - 1.0) * lr + lb_im * li) / den
    coef_im = (lb_im * lr - (lb_re - 1.0) * li) / den

    def power(tau):
        m = jnp.exp(tau * zr)
        return m * jnp.cos(tau * ang), m * jnp.sin(tau * ang)

    rg = lax.broadcasted_iota(jnp.int32, (LANES, sw), 0) // S5_GROUP_CH
    cg = lax.broadcasted_iota(jnp.int32, (LANES, sw), 1) // S5_STATE
    same = rg == cg

    def spread(ref):
        return jnp.where(same, jnp.concatenate([ref[0]] * S5_GPB, axis=0), 0.0)

    bt_re = spread(btr_ref)
    bt_im = spread(bti_ref)
    bb_re = coef_re * bt_re - coef_im * bt_im
    bb_im = coef_re * bt_im + coef_im * bt_re
    cx_re = spread(cxr_ref)
    cx_im = spread(cxi_ref)

    for t in range(el):
        pr, pi = power(float(el - 1 - t))
        rows = slice(t * LANES, (t + 1) * LANES)
        wa_ref[0, rows, el * LANES:el * LANES + sw] = (pr * bb_re - pi * bb_im).astype(wa_ref.dtype)
        wa_ref[0, rows, el * LANES + sw:el * LANES + 2 * sw] = (pr * bb_im + pi * bb_re).astype(wa_ref.dtype)

    cl = []
    for tau in range(el + 1):
        pr, pi = power(float(tau))
        cl.append((cx_re * pr - cx_im * pi, cx_re * pi + cx_im * pr))
    for t2 in range(el):
        q_re, q_im = cl[t2 + 1]
        rows = slice(t2 * LANES, (t2 + 1) * LANES)
        wot_ref[0, rows, 0:sw] = q_re.astype(wot_ref.dtype)
        wot_ref[0, rows, sw:2 * sw] = (-q_im).astype(wot_ref.dtype)

    def contract(a, b):
        return lax.dot_general(a, b, (((1,), (1,)), ((), ())), preferred_element_type=F32,
                               precision=lax.Precision.HIGHEST)

    kern = [(contract(bb_re, cl[tau][0]) - contract(bb_im, cl[tau][1])).astype(wa_ref.dtype) for tau in range(el)]
    zero = jnp.zeros((LANES, LANES), wa_ref.dtype)
    for t in range(el):
        for t2 in range(el):
            wa_ref[0, t * LANES:(t + 1) * LANES, t2 * LANES:(t2 + 1) * LANES] = kern[t2 - t] if t2 >= t else zero

    pr, pi = power(float(el))
    lam_ref[0, 0:1, :] = pr
    lam_ref[0, 1:2, :] = pi


def _s5_prep_call(lam_re, lam_im, log_step, b_re, b_im, c_re, c_im):
    g, p = lam_re.shape
    gc = b_re.shape[-1]
    el = S5_L
    row = lambda x: x.astype(F32).reshape(S5_GB, 1, S5_SW)
    bt = lambda x: x.astype(F32).reshape(S5_GB, S5_GPB, p, gc).transpose(0, 3, 1, 2).reshape(S5_GB, gc, S5_SW)
    cx = lambda x: x.astype(F32).reshape(S5_GB, S5_GPB, gc, p).transpose(0, 2, 1, 3).reshape(S5_GB, gc, S5_SW)
    vec = pl.BlockSpec((1, 1, S5_SW), lambda i: (i, 0, 0))
    mat = pl.BlockSpec((1, gc, S5_SW), lambda i: (i, 0, 0))
    wa_cols = el * LANES + 2 * S5_SW
    return pl.pallas_call(
        _s5_prep_kernel,
        grid=(S5_GB,),
        in_specs=[vec, vec, vec, mat, mat, mat, mat],
        out_specs=[pl.BlockSpec((1, el * LANES, wa_cols), lambda i: (i, 0, 0)),
                   pl.BlockSpec((1, el * LANES, 2 * S5_SW), lambda i: (i, 0, 0)),
                   pl.BlockSpec((1, 2, S5_SW), lambda i: (i, 0, 0))],
        out_shape=[jax.ShapeDtypeStruct((S5_GB, el * LANES, wa_cols), BF16),
                   jax.ShapeDtypeStruct((S5_GB, el * LANES, 2 * S5_SW), BF16),
                   jax.ShapeDtypeStruct((S5_GB, 2, S5_SW), F32)],
        compiler_params=_cparams(("arbitrary",)),
        name="s5_prep",
    )(row(lam_re), row(lam_im), row(jnp.repeat(log_step, p)), bt(b_re), bt(b_im), cx(c_re), cx(c_im))


def _s5_kernel(u_ref, wa_ref, wo_ref, lam_ref, d_ref, y_ref, s_s, xp_s, *, nrow, nb):
    el = S5_L
    sw = S5_SW
    xs = [u_ref[:, pl.ds(t, nrow, stride=el), :].reshape(nb * nrow, LANES) for t in range(el)]
    xcat = jnp.concatenate([x.astype(BF16) for x in xs], axis=1)
    r = _dot(xcat, wa_ref[0])
    s_s[...] = r[:, el * LANES:]
    lam_r = lam_ref[0, 0:1, :]
    lam_i = lam_ref[0, 1:2, :]

    def step(n, carry):
        new = []
        for j in range(nb):
            x_r, x_i = carry[2 * j], carry[2 * j + 1]
            row = pl.ds(j * nrow + n, 1)
            xp_s[row, 0:sw] = x_r
            xp_s[row, sw:2 * sw] = x_i
            new.append(lam_r * x_r - lam_i * x_i + s_s[row, 0:sw])
            new.append(lam_r * x_i + lam_i * x_r + s_s[row, sw:2 * sw])
        return tuple(new)

    zero = jnp.zeros((1, sw), F32)
    lax.fori_loop(0, nrow, step, (zero,) * (2 * nb), unroll=4)
    y = r[:, 0:el * LANES] + _dot_nt(xp_s[...].astype(BF16), wo_ref[0])
    dsk = d_ref[0]
    for t in range(el):
        y_t = y[:, t * LANES:(t + 1) * LANES] + dsk * xs[t]
        y_ref[:, pl.ds(t, nrow, stride=el), :] = y_t.reshape(nb, nrow, LANES)


def _s5_call(u, w_a, w_o, lam, d_skip):
    bsz, seq, width = u.shape
    nrow = seq // S5_L
    nb = S5_NB if bsz % S5_NB == 0 else 1
    return pl.pallas_call(
        functools.partial(_s5_kernel, nrow=nrow, nb=nb),
        grid=(S5_GB, bsz // nb),
        in_specs=[pl.BlockSpec((nb, seq, LANES), lambda g, b: (b, 0, g)),
                  pl.BlockSpec((1,) + w_a.shape[1:], lambda g, b: (g, 0, 0)),
                  pl.BlockSpec((1,) + w_o.shape[1:], lambda g, b: (g, 0, 0)),
                  pl.BlockSpec((1, 2, S5_SW), lambda g, b: (g, 0, 0)),
                  pl.BlockSpec((1, 1, LANES), lambda g, b: (g, 0, 0))],
        out_specs=pl.BlockSpec((nb, seq, LANES), lambda g, b: (b, 0, g)),
        out_shape=jax.ShapeDtypeStruct((bsz, seq, width), F32),
        scratch_shapes=2 * [pltpu.VMEM((nb * nrow, 2 * S5_SW), F32)],
        compiler_params=_cparams(("arbitrary", "arbitrary")),
        name="s5",
    )(u, w_a, w_o, lam, d_skip.reshape(S5_GB, 1, LANES))


def _gelu_tanh(x):
    return 0.5 * x * (1.0 + jnp.tanh(math.sqrt(2.0 / math.pi) * (x + 0.044715 * (x * x * x))))


def _merge_kernel(h_ref, oa_ref, ys_ref, gab_ref, mod_ref, wglu_ref, bglu_ref, wpa_ref, wpb_ref, wout_ref, o_ref):
    d = h_ref.shape[-1]
    yb = _gelu_tanh(ys_ref[0])
    yb = yb * _sigmoid(_dot(yb.astype(BF16), wglu_ref[...]) + bglu_ref[...])
    y_b = _dot(yb.astype(BF16), wpb_ref[...])
    y_a = _dot(oa_ref[0], wpa_ref[...])
    gab = gab_ref[0].astype(F32)
    merged = _sigmoid(gab[:, 0:d]) * y_a + _sigmoid(gab[:, d:2 * d]) * y_b
    o_ref[0] = h_ref[0] + mod_ref[0, 5:6, :] * _dot(merged.astype(BF16), wout_ref[...])


def _merge_call(h, o_a, y_s, gab, mod, w_glu, b_glu, w_pa, w_pb, w_out):
    bsz, seq, d = h.shape
    tm = min(MERGE_TM, seq)
    tok = lambda w: pl.BlockSpec((1, tm, w), lambda b, i: (b, i, 0))
    return pl.pallas_call(
        _merge_kernel,
        grid=(bsz, seq // tm),
        in_specs=[tok(d), tok(o_a.shape[-1]), tok(y_s.shape[-1]), tok(gab.shape[-1]),
                  pl.BlockSpec((1, N_MOD, d), lambda b, i: (b, 0, 0)),
                  _resident(w_glu.shape), _resident((1, b_glu.shape[-1])), _resident(w_pa.shape),
                  _resident(w_pb.shape), _resident(w_out.shape)],
        out_specs=tok(d),
        out_shape=jax.ShapeDtypeStruct((bsz, seq, d), F32),
        compiler_params=_cparams(("parallel", "parallel")),
        name="merge",
    )(h, o_a, y_s, gab, mod, w_glu, b_glu.reshape(1, -1), w_pa, w_pb, w_out)


def _regroup_w_in(w_in):
    w = DN_WIDTH
    ba = w_in[:, 4 * w:4 * w + 2 * DN_HEADS]
    ba = jnp.pad(ba, ((0, 0), (0, LANES - 2 * DN_HEADS)))
    s5_off = 4 * w + 2 * DN_HEADS
    return jnp.concatenate([w_in[:, 0:3 * w], w_in[:, 3 * w:4 * w], ba,
                            w_in[:, s5_off:s5_off + S5_WIDTH], w_in[:, s5_off + S5_WIDTH:]], axis=1).astype(BF16)


def kernel(x, c, w_ada, b_ada, g_ffn1, w1_ffn1, w3_ffn1, w2_ffn1, g_mix, w_in, conv_qkv, a_log, dt_bias, g_onorm, lam_re, lam_im, log_step, b_re, b_im, c_re, c_im, d_skip, w_glu, b_glu, w_proj_a, w_proj_b, w_out, g_ffn2, w1_ffn2, w3_ffn2, w2_ffn2, g_final):
    bsz, seq, d = x.shape
    depth = w_ada.shape[0]
    h = x
    for l in range(depth):
        last = l == depth - 1
        mod = _mod_call(c, w_ada[l], b_ada[l]).reshape(bsz, N_MOD, d)
        h = _ffn_call(h, mod, g_ffn1[l], w1_ffn1[l].astype(BF16), w3_ffn1[l].astype(BF16),
                      w2_ffn1[l].astype(BF16), g_final, row=0, final=False)
        qkv, z, ba, s5_in, gab = _inproj_call(h, mod, g_mix[l], _regroup_w_in(w_in[l]), conv_qkv[l])
        hp = jnp.zeros((SUBLANES, LANES), F32)
        hp = hp.at[0, DN_HEADS:2 * DN_HEADS].set(a_log[l]).at[1, DN_HEADS:2 * DN_HEADS].set(dt_bias[l])
        gon = jnp.tile(g_onorm[l], DN_HEADS).reshape(1, DN_WIDTH)
        o_a = _dn_call(qkv, ba, z, hp, gon)
        w_a, w_o, lam = _s5_prep_call(lam_re[l], lam_im[l], log_step[l], b_re[l], b_im[l], c_re[l], c_im[l])
        y_s = _s5_call(s5_in, w_a, w_o, lam, d_skip[l])
        h = _merge_call(h, o_a, y_s, gab, mod, w_glu[l].astype(BF16), b_glu[l], w_proj_a[l].astype(BF16),
                        w_proj_b[l].astype(BF16), w_out[l].astype(BF16))
        h = _ffn_call(h, mod, g_ffn2[l], w1_ffn2[l].astype(BF16), w3_ffn2[l].astype(BF16),
                      w2_ffn2[l].astype(BF16), g_final, row=6, final=last)
    if depth == 0:
        raise ValueError("depth must be positive")
    return h
```

```python
import functools
import math

import jax
import jax.numpy as jnp
from jax import lax
from jax.experimental import pallas as pl
from jax.experimental.pallas import tpu as pltpu

F32 = jnp.float32
BF16 = jnp.bfloat16
EPS = 1e-6

LANES = 128
SUBLANES = 8
MXU_N = 256
VMEM_LIMIT_BYTES = 56 * 1024 * 1024

N_MOD = 9
DN_HEADS = 8
DN_HEAD_DIM = 64
DN_WIDTH = DN_HEADS * DN_HEAD_DIM
CONV_WIDTH = 4
CHUNK = 64
N_PAIRS = DN_WIDTH // LANES
S5_GROUP_CH = 16
S5_STATE = 64
S5_WIDTH = 512
S5_GB = S5_WIDTH // LANES
S5_GPB = LANES // S5_GROUP_CH
S5_L = 8
S5_SW = S5_GPB * S5_STATE
S5_NB = 4

FFN_TM = 1024
FFN_SPLIT = 2
INPROJ_TM = 512
INPROJ_CB = 256
DN_TS = 256
DN_NB = 4
MERGE_TM = 1024


def _dot(a, b):
    return jnp.dot(a, b, preferred_element_type=F32)


def _dot_nt(a, b):
    return lax.dot_general(a, b, (((1,), (1,)), ((), ())), preferred_element_type=F32)


def _dot_tn(a, b):
    return lax.dot_general(a, b, (((0,), (0,)), ((), ())), preferred_element_type=F32)


def _split3(x):
    hi = x.astype(BF16)
    r = x - hi.astype(F32)
    mid = r.astype(BF16)
    lo = (r - mid.astype(F32)).astype(BF16)
    return hi, mid, lo


def _dot_sel_r(x, sel):
    hi = x.astype(BF16)
    lo = (x - hi.astype(F32)).astype(BF16)
    return _dot(hi, sel) + _dot(lo, sel)


def _sigmoid(x):
    return 0.5 * jnp.tanh(0.5 * x) + 0.5


def _silu(x):
    return x * _sigmoid(x)


def _norm_mod(x, gain, shift, scale):
    ms = jnp.mean(x * x, axis=-1, keepdims=True)
    y = x * lax.rsqrt(ms + EPS) * gain
    return y * (1.0 + scale) + shift


def _cparams(sem):
    return pltpu.CompilerParams(dimension_semantics=sem, vmem_limit_bytes=VMEM_LIMIT_BYTES)


def _resident(shape):
    nd = len(shape)
    return pl.BlockSpec(shape, lambda *_: (0,) * nd, pipeline_mode=pl.Buffered(1))


def _mod_kernel(c_ref, w_ref, b_ref, o_ref):
    a = _silu(c_ref[...])
    o_ref[...] = jnp.dot(a, w_ref[...], preferred_element_type=F32,
                         precision=lax.Precision.HIGHEST) + b_ref[...]


def _mod_call(c, w_ada, b_ada):
    bsz, d = c.shape
    n = w_ada.shape[1]
    return pl.pallas_call(
        _mod_kernel,
        grid=(n // d,),
        in_specs=[pl.BlockSpec((bsz, d), lambda j: (0, 0)),
                  pl.BlockSpec((d, d), lambda j: (0, j)),
                  pl.BlockSpec((1, d), lambda j: (0, j))],
        out_specs=pl.BlockSpec((bsz, d), lambda j: (0, j)),
        out_shape=jax.ShapeDtypeStruct((bsz, n), F32),
        compiler_params=_cparams(("arbitrary",)),
        name="mod",
    )(c, w_ada, b_ada.reshape(1, n))


def _ffn_kernel(x_ref, mod_ref, g_ref, w1_ref, w3_ref, w2_ref, gf_ref, o_ref, *, row, final):
    x = x_ref[0]
    shift = mod_ref[0, row:row + 1, :]
    scale = mod_ref[0, row + 1:row + 2, :]
    gate = mod_ref[0, row + 2:row + 3, :]
    u = _norm_mod(x, g_ref[...], shift, scale).astype(BF16)
    f = w1_ref.shape[1]
    n_tiles = f // MXU_N
    bounds = [MXU_N * ((n_tiles * s + FFN_SPLIT - 1) // FFN_SPLIT) for s in range(FFN_SPLIT)] + [f]
    y = None
    for s in range(FFN_SPLIT):
        cols = slice(bounds[s], bounds[s + 1])
        a = (_silu(_dot(u, w1_ref[:, cols])) * _dot(u, w3_ref[:, cols])).astype(BF16)
        part = _dot(a, w2_ref[cols, :])
        y = part if y is None else y + part
    h = x + 0.5 * gate * y
    if final:
        ms = jnp.mean(h * h, axis=-1, keepdims=True)
        h = h * lax.rsqrt(ms + EPS) * gf_ref[...]
    o_ref[0] = h


def _ffn_call(x, mod, gain, w1, w3, w2, g_final, *, row, final):
    bsz, seq, d = x.shape
    f = w1.shape[1]
    tm = min(FFN_TM, seq)
    return pl.pallas_call(
        functools.partial(_ffn_kernel, row=row, final=final),
        grid=(bsz, seq // tm),
        in_specs=[pl.BlockSpec((1, tm, d), lambda b, i: (b, i, 0)),
                  pl.BlockSpec((1, N_MOD, d), lambda b, i: (b, 0, 0)),
                  _resident((1, d)), _resident((d, f)), _resident((d, f)), _resident((f, d)),
                  _resident((1, d))],
        out_specs=pl.BlockSpec((1, tm, d), lambda b, i: (b, i, 0)),
        out_shape=jax.ShapeDtypeStruct((bsz, seq, d), F32),
        compiler_params=_cparams(("parallel", "parallel")),
        name="ffn_final" if final else "ffn",
    )(x, mod, gain.reshape(1, d), w1, w3, w2, g_final.reshape(1, d))


_W_QKV = 3 * DN_WIDTH
_OFF_Z = _W_QKV
_OFF_BA = _OFF_Z + DN_WIDTH
_OFF_S5 = _OFF_BA + LANES
_OFF_G = _OFF_S5 + S5_WIDTH


def _inproj_kernel(h_ref, mod_ref, g_ref, w_ref, cw_ref, qkv_ref, z_ref, ba_ref, s5_ref, gab_ref, *xbufs, tm):
    @pl.when(pl.program_id(1) == 0)
    def _():
        for xbuf in xbufs:
            xbuf[0:SUBLANES, :] = jnp.zeros((SUBLANES, INPROJ_CB), F32)

    u = _norm_mod(h_ref[0], g_ref[...], mod_ref[0, 3:4, :], mod_ref[0, 4:5, :]).astype(BF16)

    def project(c0):
        xbuf = xbufs[c0 // INPROJ_CB]
        xbuf[SUBLANES:SUBLANES + tm, :] = _dot(u, w_ref[:, c0:c0 + INPROJ_CB])

    def conv_silu(c0, l0):
        xbuf = xbufs[c0 // INPROJ_CB]
        lanes = slice(l0, l0 + LANES)
        cols = slice(c0 + l0, c0 + l0 + LANES)
        acc = cw_ref[CONV_WIDTH - 1:CONV_WIDTH, cols] * xbuf[SUBLANES:SUBLANES + tm, lanes]
        for t in range(CONV_WIDTH - 1):
            off = SUBLANES - (CONV_WIDTH - 1) + t
            acc = acc + cw_ref[t:t + 1, cols] * xbuf[off:off + tm, lanes]
        xbuf[0:SUBLANES, lanes] = xbuf[tm:tm + SUBLANES, lanes]
        qkv_ref[0, :, cols] = acc * _sigmoid(acc)

    def plain(out_ref, off, c0, width):
        out_ref[0, :, c0:c0 + width] = _dot(u, w_ref[:, off + c0:off + c0 + width]).astype(out_ref.dtype)

    others = []
    for out_ref, off, total in ((z_ref, _OFF_Z, DN_WIDTH), (ba_ref, _OFF_BA, LANES),
                                (s5_ref, _OFF_S5, S5_WIDTH), (gab_ref, _OFF_G, gab_ref.shape[-1])):
        for c0 in range(0, total, INPROJ_CB):
            others.append(functools.partial(plain, out_ref, off, c0, min(INPROJ_CB, total - c0)))
    matmuls = []
    for c0 in range(0, _W_QKV, INPROJ_CB):
        matmuls.append(functools.partial(project, c0))
        matmuls.append(others.pop(0))
    matmuls += others
    convs = [functools.partial(conv_silu, c0, l0)
             for c0 in range(0, _W_QKV, INPROJ_CB) for l0 in range(0, INPROJ_CB, LANES)]
    lag = 2
    for n, task in enumerate(matmuls):
        task()
        if n >= lag and convs:
            convs.pop(0)()
    for piece in convs:
        piece()


def _inproj_call(h, mod, gain, w_cat, conv_w):
    bsz, seq, d = h.shape
    tm = min(INPROJ_TM, seq)
    wg = w_cat.shape[1] - _OFF_G
    widths = (_W_QKV, DN_WIDTH, LANES, S5_WIDTH, wg)
    dtypes = (F32, BF16, F32, F32, BF16)
    return pl.pallas_call(
        functools.partial(_inproj_kernel, tm=tm),
        grid=(bsz, seq // tm),
        in_specs=[pl.BlockSpec((1, tm, d), lambda b, i: (b, i, 0)),
                  pl.BlockSpec((1, N_MOD, d), lambda b, i: (b, 0, 0)),
                  _resident((1, d)), _resident(w_cat.shape), _resident((CONV_WIDTH, _W_QKV))],
        out_specs=[pl.BlockSpec((1, tm, w), lambda b, i: (b, i, 0)) for w in widths],
        out_shape=[jax.ShapeDtypeStruct((bsz, seq, w), t) for w, t in zip(widths, dtypes)],
        scratch_shapes=(_W_QKV // INPROJ_CB) * [pltpu.VMEM((tm + SUBLANES, INPROJ_CB), F32)],
        compiler_params=_cparams(("parallel", "arbitrary")),
        name="inproj",
    )(h, mod, gain.reshape(1, d), w_cat, conv_w)


def _seg_sum(x, seg):
    outs = [_dot(x[:, p * LANES:(p + 1) * LANES].astype(BF16), seg) for p in range(N_PAIRS)]
    return jnp.concatenate(outs, axis=1)


class _PerElement:
    def __init__(self, refs):
        self.refs = refs

    def __getitem__(self, idx):
        if isinstance(idx, tuple):
            return self.refs[idx[0]][idx[1:]]
        return self.refs[idx][...]

    def __setitem__(self, idx, val):
        if isinstance(idx, tuple):
            self.refs[idx[0]][idx[1:]] = val
        else:
            self.refs[idx][...] = val


_DN_N_SCRATCH = 14


def _dn_kernel(qkv_ref, ba_ref, z_ref, hp_ref, gon_ref, o_ref, *scratch, ts, nb):
    w = DN_WIDTH
    hd = DN_HEAD_DIM
    n_chunks = ts // CHUNK
    (st_ref, qn_s, kn_s, kb_s, kbg_s, vb_s, qd_s, kd_s, w_s, attn_s, gc_s, egl_s, o_s, u_s) = [
        _PerElement([scratch[j * _DN_N_SCRATCH + i] for j in range(nb)]) for i in range(_DN_N_SCRATCH)]

    @pl.when(pl.program_id(1) == 0)
    def _():
        for j in range(nb):
            st_ref[j] = jnp.zeros((N_PAIRS, LANES, LANES), F32)

    r128 = lax.broadcasted_iota(jnp.int32, (LANES, LANES), 0)
    c128 = lax.broadcasted_iota(jnp.int32, (LANES, LANES), 1)
    bd_mask = (r128 // hd) == (c128 // hd)
    seg = jnp.where(bd_mask, 1.0, 0.0).astype(BF16)
    er = lax.broadcasted_iota(jnp.int32, (LANES, w), 0)
    ec = lax.broadcasted_iota(jnp.int32, (LANES, w), 1) // hd
    exp_beta = jnp.where(er == ec, 1.0, 0.0).astype(BF16)
    exp_alpha = jnp.where(er - DN_HEADS == ec, 1.0, 0.0).astype(BF16)
    tr = lax.broadcasted_iota(jnp.int32, (ts, ts), 0)
    tc = lax.broadcasted_iota(jnp.int32, (ts, ts), 1)
    same_chunk = (tr // CHUNK) == (tc // CHUNK)
    tri_cum = jnp.where(same_chunk & (tr >= tc), 1.0, 0.0).astype(BF16)

    ri = lax.broadcasted_iota(jnp.int32, (CHUNK, LANES), 0)
    li = lax.broadcasted_iota(jnp.int32, (CHUNK, LANES), 1)
    jj = li % hd
    causal = ri >= jj
    strict = ri > jj
    eye = ri == jj
    eye_f = jnp.where(eye, 1.0, 0.0)
    in16 = (ri // 16) == (jj // 16)
    in32 = (ri // 32) == (jj // 32)
    in32_only = in32 & jnp.logical_not(in16)
    head0 = li < hd
    zero_b = jnp.zeros((), BF16)

    def bd(y):
        yb = y.astype(BF16)
        return jnp.where(bd_mask, jnp.concatenate([yb, yb], axis=0), zero_b)

    def mm(x, y):
        return _dot(x.astype(BF16), bd(y))

    def tile(j, c, p):
        return (j, slice(c * CHUNK, (c + 1) * CHUNK), slice(p * LANES, (p + 1) * LANES))


    def prepare(j):
        ba = ba_ref[j]
        beta = _sigmoid(ba)
        xa = ba + hp_ref[1:2, :]
        softplus = jnp.maximum(xa, 0.0) + jnp.log1p(jnp.exp(-jnp.abs(xa)))
        log_alpha = -jnp.exp(hp_ref[0:1, :]) * softplus
        la_hi, la_mid, la_lo = _split3(log_alpha)
        g_cum = _dot(tri_cum, la_hi) + _dot(tri_cum, la_mid) + _dot(tri_cum, la_lo)
        yield
        beta_f = _dot(beta.astype(BF16), exp_beta)
        yield
        gc_f = _dot_sel_r(g_cum, exp_alpha)
        yield
        q = qkv_ref[j, :, 0:w]
        qn = q * (lax.rsqrt(_seg_sum(q * q, seg) + EPS) * (hd ** -0.5))
        qn_s[j] = qn.astype(BF16)
        yield
        k = qkv_ref[j, :, w:2 * w]
        kn = k * lax.rsqrt(_seg_sum(k * k, seg) + EPS)
        kn_s[j] = kn.astype(BF16)
        yield
        eg = jnp.exp(gc_f)
        qd_s[j] = (qn * eg).astype(BF16)
        gc_s[j] = gc_f
        yield
        kb = kn * beta_f
        kb_s[j] = kb.astype(BF16)
        kbg_s[j] = (kb * eg).astype(BF16)
        yield
        for c in range(n_chunks):
            rows = slice(c * CHUNK, (c + 1) * CHUNK)
            g_tot = gc_s[j, (c + 1) * CHUNK - 1:(c + 1) * CHUNK, :]
            kd_s[j, rows, :] = (kn[rows] * jnp.exp(g_tot - gc_f[rows])).astype(BF16)
            egl_s[j, c * SUBLANES:c * SUBLANES + 1, :] = jnp.exp(g_tot)
        yield
        vb_s[j] = (qkv_ref[j, :, 2 * w:3 * w] * beta_f).astype(BF16)
        yield

    def solve(j):
        items = [tile(j, c, p) for c in range(n_chunks) for p in range(N_PAIRS)]
        kn_c = [kn_s[it] for it in items]
        k2 = [jnp.concatenate([jnp.where(head0, x, zero_b), jnp.where(head0, zero_b, x)], axis=0) for x in kn_c]
        p2 = [_dot_nt(jnp.concatenate([kb_s[it], qn_s[it]], axis=0), y) for it, y in zip(items, k2)]
        yield
        a_mat = []
        for it, pp in zip(items, p2):
            gc_c = gc_s[it]
            gc_row = jnp.sum(jnp.where(eye, gc_c, 0.0), axis=0, keepdims=True)
            decay = jnp.exp(jnp.where(causal, gc_c - gc_row, -1e30))
            a_mat.append(jnp.where(strict, pp[0:CHUNK] * decay, 0.0))
            attn_s[it] = (pp[CHUNK:2 * CHUNK] * decay).astype(BF16)
        yield
        d = [jnp.where(in16, a, 0.0) for a in a_mat]
        d2 = [mm(x, x) for x in d]
        yield
        imd = [eye_f - x for x in d]
        both = [mm(jnp.concatenate([x, y], axis=0), x) for x, y in zip(d2, imd)]
        d4 = [b[0:CHUNK] for b in both]
        p1 = [y + b[CHUNK:2 * CHUNK] for y, b in zip(imd, both)]
        yield
        both = [mm(jnp.concatenate([x, y], axis=0), x) for x, y in zip(d4, p1)]
        d8 = [b[0:CHUNK] for b in both]
        q1 = [y + b[CHUNK:2 * CHUNK] for y, b in zip(p1, both)]
        yield
        t16 = [x + mm(x, y) for x, y in zip(q1, d8)]
        yield
        f32_ = [mm(t, jnp.where(in32_only, a, 0.0)) for t, a in zip(t16, a_mat)]
        yield
        t32 = [t - mm(f, t) for t, f in zip(t16, f32_)]
        yield
        f64_ = [mm(t, jnp.where(in32, 0.0, a)) for t, a in zip(t32, a_mat)]
        yield
        t64 = [t - mm(f, t) for t, f in zip(t32, f64_)]
        yield
        for it, t in zip(items, t64):
            rhs = jnp.concatenate([bd(vb_s[it]), bd(kbg_s[it])], axis=1)
            uw = _dot(t.astype(BF16), rhs)
            u_s[it] = uw[:, 0:LANES]
            w_s[it] = uw[:, LANES:2 * LANES].astype(BF16)
        yield

    def recur(j):
        state = [st_ref[j, p] for p in range(N_PAIRS)]
        for c in range(n_chunks):
            its = [tile(j, c, p) for p in range(N_PAIRS)]
            rs = [_dot(jnp.concatenate([w_s[it], qd_s[it]], axis=0), s.astype(BF16))
                  for it, s in zip(its, state)]
            yield
            v_new = [u_s[it] - r[0:CHUNK] for it, r in zip(its, rs)]
            upd = [_dot_tn(kd_s[it], v.astype(BF16)) for it, v in zip(its, v_new)]
            o_c = [r[CHUNK:2 * CHUNK] + _dot(attn_s[it], bd(v)) for it, r, v in zip(its, rs, v_new)]
            yield
            state = [s * egl_s[j, c * SUBLANES:c * SUBLANES + 1, it[2]] + jnp.where(bd_mask, x, 0.0)
                     for it, s, x in zip(its, state, upd)]
            for it, o in zip(its, o_c):
                o_s[it] = o
            yield
        for p in range(N_PAIRS):
            st_ref[j, p] = state[p]

    def finish(j):
        for c in range(n_chunks):
            o = o_s[j, c * CHUNK:(c + 1) * CHUNK, :]
            ms = _seg_sum(o * o, seg) * (1.0 / hd)
            zf = z_ref[j, c * CHUNK:(c + 1) * CHUNK, :].astype(F32)
            out = o * lax.rsqrt(ms + EPS) * gon_ref[...] * (zf * _sigmoid(zf))
            o_ref[j, c * CHUNK:(c + 1) * CHUNK, :] = out.astype(o_ref.dtype)
            yield

    for phase in (prepare, solve, recur, finish):
        _run_interleaved([phase(j) for j in range(nb)])


def _run_interleaved(gens):
    gens = list(gens)
    while gens:
        for g in list(gens):
            try:
                next(g)
            except StopIteration:
                gens.remove(g)


def _dn_call(qkv, ba, z, hp, gon):
    bsz, seq, _ = qkv.shape
    ts = min(DN_TS, seq)
    nb = DN_NB if bsz % DN_NB == 0 else 1
    w = DN_WIDTH
    bf = lambda: pltpu.VMEM((ts, w), BF16)
    ff = lambda: pltpu.VMEM((ts, w), F32)
    tok = lambda width: pl.BlockSpec((nb, ts, width), lambda b, i: (b, i, 0))
    return pl.pallas_call(
        functools.partial(_dn_kernel, ts=ts, nb=nb),
        grid=(bsz // nb, seq // ts),
        in_specs=[tok(3 * w), tok(LANES), tok(w), _resident((SUBLANES, LANES)), _resident((1, w))],
        out_specs=tok(w),
        out_shape=jax.ShapeDtypeStruct((bsz, seq, w), BF16),
        scratch_shapes=nb * [pltpu.VMEM((N_PAIRS, LANES, LANES), F32),
                             bf(), bf(), bf(), bf(), bf(), bf(), bf(), bf(), bf(), ff(),
                             pltpu.VMEM((ts // CHUNK * SUBLANES, w), F32), ff(), ff()],
        compiler_params=_cparams(("parallel", "arbitrary")),
        name="deltanet",
    )(qkv, ba, z, hp, gon)


def _s5_prep_kernel(lre_ref, lim_ref, lstep_ref, btr_ref, bti_ref, cxr_ref, cxi_ref, wa_ref, wot_ref, lam_ref):
    el = S5_L
    sw = S5_SW
    lr = jnp.minimum(lre_ref[0], -1e-4)
    li = lim_ref[0]
    step = jnp.exp(lstep_ref[0])
    zr = lr * step
    ang = li * step
    mag = jnp.exp(zr)
    lb_re = mag * jnp.cos(ang)
    lb_im = mag * jnp.sin(ang)
    den = lr * lr + li * li
    coef_re = ((lb_re - 1.0) * lr + lb_im * li) / den
    coef_im = (lb_im * lr - (lb_re - 1.0) * li) / den

    def power(tau):
        m = jnp.exp(tau * zr)
        return m * jnp.cos(tau * ang), m * jnp.sin(tau * ang)

    rg = lax.broadcasted_iota(jnp.int32, (LANES, sw), 0) // S5_GROUP_CH
    cg = lax.broadcasted_iota(jnp.int32, (LANES, sw), 1) // S5_STATE
    same = rg == cg

    def spread(ref):
        return jnp.where(same, jnp.concatenate([ref[0]] * S5_GPB, axis=0), 0.0)

    bt_re = spread(btr_ref)
    bt_im = spread(bti_ref)
    bb_re = coef_re * bt_re - coef_im * bt_im
    bb_im = coef_re * bt_im + coef_im * bt_re
    cx_re = spread(cxr_ref)
    cx_im = spread(cxi_ref)

    for t in range(el):
        pr, pi = power(float(el - 1 - t))
        rows = slice(t * LANES, (t + 1) * LANES)
        wa_ref[0, rows, el * LANES:el * LANES + sw] = (pr * bb_re - pi * bb_im).astype(wa_ref.dtype)
        wa_ref[0, rows, el * LANES + sw:el * LANES + 2 * sw] = (pr * bb_im + pi * bb_re).astype(wa_ref.dtype)

    cl = []
    for tau in range(el + 1):
        pr, pi = power(float(tau))
        cl.append((cx_re * pr - cx_im * pi, cx_re * pi + cx_im * pr))
    for t2 in range(el):
        q_re, q_im = cl[t2 + 1]
        rows = slice(t2 * LANES, (t2 + 1) * LANES)
        wot_ref[0, rows, 0:sw] = q_re.astype(wot_ref.dtype)
        wot_ref[0, rows, sw:2 * sw] = (-q_im).astype(wot_ref.dtype)

    def contract(a, b):
        return lax.dot_general(a, b, (((1,), (1,)), ((), ())), preferred_element_type=F32,
                               precision=lax.Precision.HIGHEST)

    kern = [(contract(bb_re, cl[tau][0]) - contract(bb_im, cl[tau][1])).astype(wa_ref.dtype) for tau in range(el)]
    zero = jnp.zeros((LANES, LANES), wa_ref.dtype)
    for t in range(el):
        for t2 in range(el):
            wa_ref[0, t * LANES:(t + 1) * LANES, t2 * LANES:(t2 + 1) * LANES] = kern[t2 - t] if t2 >= t else zero

    pr, pi = power(float(el))
    lam_ref[0, 0:1, :] = pr
    lam_ref[0, 1:2, :] = pi


def _s5_prep_call(lam_re, lam_im, log_step, b_re, b_im, c_re, c_im):
    g, p = lam_re.shape
    gc = b_re.shape[-1]
    el = S5_L
    row = lambda x: x.astype(F32).reshape(S5_GB, 1, S5_SW)
    bt = lambda x: x.astype(F32).reshape(S5_GB, S5_GPB, p, gc).transpose(0, 3, 1, 2).reshape(S5_GB, gc, S5_SW)
    cx = lambda x: x.astype(F32).reshape(S5_GB, S5_GPB, gc, p).transpose(0, 2, 1, 3).reshape(S5_GB, gc, S5_SW)
    vec = pl.BlockSpec((1, 1, S5_SW), lambda i: (i, 0, 0))
    mat = pl.BlockSpec((1, gc, S5_SW), lambda i: (i, 0, 0))
    wa_cols = el * LANES + 2 * S5_SW
    return pl.pallas_call(
        _s5_prep_kernel,
        grid=(S5_GB,),
        in_specs=[vec, vec, vec, mat, mat, mat, mat],
        out_specs=[pl.BlockSpec((1, el * LANES, wa_cols), lambda i: (i, 0, 0)),
                   pl.BlockSpec((1, el * LANES, 2 * S5_SW), lambda i: (i, 0, 0)),
                   pl.BlockSpec((1, 2, S5_SW), lambda i: (i, 0, 0))],
        out_shape=[jax.ShapeDtypeStruct((S5_GB, el * LANES, wa_cols), BF16),
                   jax.ShapeDtypeStruct((S5_GB, el * LANES, 2 * S5_SW), BF16),
                   jax.ShapeDtypeStruct((S5_GB, 2, S5_SW), F32)],
        compiler_params=_cparams(("arbitrary",)),
        name="s5_prep",
    )(row(lam_re), row(lam_im), row(jnp.repeat(log_step, p)), bt(b_re), bt(b_im), cx(c_re), cx(c_im))


def _s5_kernel(u_ref, wa_ref, wo_ref, lam_ref, d_ref, y_ref, s_s, xp_s, *, nrow, nb):
    el = S5_L
    sw = S5_SW
    xs = [u_ref[:, pl.ds(t, nrow, stride=el), :].reshape(nb * nrow, LANES) for t in range(el)]
    xcat = jnp.concatenate([x.astype(BF16) for x in xs], axis=1)
    r = _dot(xcat, wa_ref[0])
    s_s[...] = r[:, el * LANES:]
    lam_r = lam_ref[0, 0:1, :]
    lam_i = lam_ref[0, 1:2, :]

    def step(n, carry):
        new = []
        for j in range(nb):
            x_r, x_i = carry[2 * j], carry[2 * j + 1]
            row = pl.ds(j * nrow + n, 1)
            xp_s[row, 0:sw] = x_r
            xp_s[row, sw:2 * sw] = x_i
            new.append(lam_r * x_r - lam_i * x_i + s_s[row, 0:sw])
            new.append(lam_r * x_i + lam_i * x_r + s_s[row, sw:2 * sw])
        return tuple(new)

    zero = jnp.zeros((1, sw), F32)
    lax.fori_loop(0, nrow, step, (zero,) * (2 * nb), unroll=4)
    y = r[:, 0:el * LANES] + _dot_nt(xp_s[...].astype(BF16), wo_ref[0])
    dsk = d_ref[0]
    for t in range(el):
        y_t = y[:, t * LANES:(t + 1) * LANES] + dsk * xs[t]
        y_ref[:, pl.ds(t, nrow, stride=el), :] = y_t.reshape(nb, nrow, LANES)


def _s5_call(u, w_a, w_o, lam, d_skip):
    bsz, seq, width = u.shape
    nrow = seq // S5_L
    nb = S5_NB if bsz % S5_NB == 0 else 1
    return pl.pallas_call(
        functools.partial(_s5_kernel, nrow=nrow, nb=nb),
        grid=(S5_GB, bsz // nb),
        in_specs=[pl.BlockSpec((nb, seq, LANES), lambda g, b: (b, 0, g)),
                  pl.BlockSpec((1,) + w_a.shape[1:], lambda g, b: (g, 0, 0)),
                  pl.BlockSpec((1,) + w_o.shape[1:], lambda g, b: (g, 0, 0)),
                  pl.BlockSpec((1, 2, S5_SW), lambda g, b: (g, 0, 0)),
                  pl.BlockSpec((1, 1, LANES), lambda g, b: (g, 0, 0))],
        out_specs=pl.BlockSpec((nb, seq, LANES), lambda g, b: (b, 0, g)),
        out_shape=jax.ShapeDtypeStruct((bsz, seq, width), F32),
        scratch_shapes=2 * [pltpu.VMEM((nb * nrow, 2 * S5_SW), F32)],
        compiler_params=_cparams(("arbitrary", "arbitrary")),
        name="s5",
    )(u, w_a, w_o, lam, d_skip.reshape(S5_GB, 1, LANES))


def _gelu_tanh(x):
    return 0.5 * x * (1.0 + jnp.tanh(math.sqrt(2.0 / math.pi) * (x + 0.044715 * (x * x * x))))


def _merge_kernel(h_ref, oa_ref, ys_ref, gab_ref, mod_ref, wglu_ref, bglu_ref, wpa_ref, wpb_ref, wout_ref, o_ref):
    d = h_ref.shape[-1]
    yb = _gelu_tanh(ys_ref[0])
    yb = yb * _sigmoid(_dot(yb.astype(BF16), wglu_ref[...]) + bglu_ref[...])
    y_b = _dot(yb.astype(BF16), wpb_ref[...])
    y_a = _dot(oa_ref[0], wpa_ref[...])
    gab = gab_ref[0].astype(F32)
    merged = _sigmoid(gab[:, 0:d]) * y_a + _sigmoid(gab[:, d:2 * d]) * y_b
    o_ref[0] = h_ref[0] + mod_ref[0, 5:6, :] * _dot(merged.astype(BF16), wout_ref[...])


def _merge_call(h, o_a, y_s, gab, mod, w_glu, b_glu, w_pa, w_pb, w_out):
    bsz, seq, d = h.shape
    tm = min(MERGE_TM, seq)
    tok = lambda w: pl.BlockSpec((1, tm, w), lambda b, i: (b, i, 0))
    return pl.pallas_call(
        _merge_kernel,
        grid=(bsz, seq // tm),
        in_specs=[tok(d), tok(o_a.shape[-1]), tok(y_s.shape[-1]), tok(gab.shape[-1]),
                  pl.BlockSpec((1, N_MOD, d), lambda b, i: (b, 0, 0)),
                  _resident(w_glu.shape), _resident((1, b_glu.shape[-1])), _resident(w_pa.shape),
                  _resident(w_pb.shape), _resident(w_out.shape)],
        out_specs=tok(d),
        out_shape=jax.ShapeDtypeStruct((bsz, seq, d), F32),
        compiler_params=_cparams(("parallel", "parallel")),
        name="merge",
    )(h, o_a, y_s, gab, mod, w_glu, b_glu.reshape(1, -1), w_pa, w_pb, w_out)


def _regroup_w_in(w_in):
    w = DN_WIDTH
    ba = w_in[:, 4 * w:4 * w + 2 * DN_HEADS]
    ba = jnp.pad(ba, ((0, 0), (0, LANES - 2 * DN_HEADS)))
    s5_off = 4 * w + 2 * DN_HEADS
    return jnp.concatenate([w_in[:, 0:3 * w], w_in[:, 3 * w:4 * w], ba,
                            w_in[:, s5_off:s5_off + S5_WIDTH], w_in[:, s5_off + S5_WIDTH:]], axis=1).astype(BF16)


def kernel(x, c, w_ada, b_ada, g_ffn1, w1_ffn1, w3_ffn1, w2_ffn1, g_mix, w_in, conv_qkv, a_log, dt_bias, g_onorm, lam_re, lam_im, log_step, b_re, b_im, c_re, c_im, d_skip, w_glu, b_glu, w_proj_a, w_proj_b, w_out, g_ffn2, w1_ffn2, w3_ffn2, w2_ffn2, g_final):
    bsz, seq, d = x.shape
    depth = w_ada.shape[0]
    h = x
    for l in range(depth):
        last = l == depth - 1
        mod = _mod_call(c, w_ada[l], b_ada[l]).reshape(bsz, N_MOD, d)
        h = _ffn_call(h, mod, g_ffn1[l], w1_ffn1[l].astype(BF16), w3_ffn1[l].astype(BF16),
                      w2_ffn1[l].astype(BF16), g_final, row=0, final=False)
        qkv, z, ba, s5_in, gab = _inproj_call(h, mod, g_mix[l], _regroup_w_in(w_in[l]), conv_qkv[l])
        hp = jnp.zeros((SUBLANES, LANES), F32)
        hp = hp.at[0, DN_HEADS:2 * DN_HEADS].set(a_log[l]).at[1, DN_HEADS:2 * DN_HEADS].set(dt_bias[l])
        gon = jnp.tile(g_onorm[l], DN_HEADS).reshape(1, DN_WIDTH)
        o_a = _dn_call(qkv, ba, z, hp, gon)
        w_a, w_o, lam = _s5_prep_call(lam_re[l], lam_im[l], log_step[l], b_re[l], b_im[l], c_re[l], c_im[l])
        y_s = _s5_call(s5_in, w_a, w_o, lam, d_skip[l])
        h = _merge_call(h, o_a, y_s, gab, mod, w_glu[l].astype(BF16), b_glu[l], w_proj_a[l].astype(BF16),
                        w_proj_b[l].astype(BF16), w_out[l].astype(BF16))
        h = _ffn_call(h, mod, g_ffn2[l], w1_ffn2[l].astype(BF16), w3_ffn2[l].astype(BF16),
                      w2_ffn2[l].astype(BF16), g_final, row=6, final=last)
    if depth == 0:
        raise ValueError("depth must be positive")
    return h
```
